```python
import jax, jax.numpy as jnp
from jax import lax
import numpy as np

D_MODEL = 1024
BATCH = 8
SEQ = 4096
DEPTH = 1

CHUNK = 64
CONV_WIDTH = D_MODEL // 2
CONV_KERNEL = 31
HG_WIDTH = D_MODEL // 2
HG_HEAD_DIM = 128
HG_HEADS = HG_WIDTH // HG_HEAD_DIM
D_FF = 4 * D_MODEL
N_BRANCHES = 2
NORM_EPS = 1e-6
LN_EPS = 1e-5
IN_COLS = 2 * CONV_WIDTH + 4 * HG_WIDTH + N_BRANCHES * D_MODEL

kernel_name = "conv_hgrn2_gated_hybrid_block"


def rms_norm(x, w):
    xf = x.astype(jnp.float32)
    y = xf * lax.rsqrt(jnp.mean(xf * xf, axis=-1, keepdims=True) + NORM_EPS)
    return (y * w.astype(jnp.float32)).astype(x.dtype)


def layer_norm(x, w, b):
    xf = x.astype(jnp.float32)
    mu = jnp.mean(xf, axis=-1, keepdims=True)
    xc = xf - mu
    var = jnp.mean(xc * xc, axis=-1, keepdims=True)
    y = xc * lax.rsqrt(var + LN_EPS) * w.astype(jnp.float32) + b.astype(jnp.float32)
    return y.astype(x.dtype)


def split_points():
    sizes = [CONV_WIDTH, CONV_WIDTH, HG_WIDTH, HG_WIDTH, HG_WIDTH, HG_WIDTH, D_MODEL]
    pts, acc = [], 0
    for s in sizes:
        acc += s
        pts.append(acc)
    return pts


def conformer_conv(a, a_gate, dw_w, dw_b, ln_w, ln_b, w_pw, b_pw):
    u = a * jax.nn.sigmoid(a_gate)
    u = lax.conv_general_dilated(
        u, dw_w.astype(u.dtype), window_strides=(1,),
        padding=[(CONV_KERNEL - 1, 0)],
        dimension_numbers=("NWC", "WIO", "NWC"),
        feature_group_count=CONV_WIDTH) + dw_b
    u = jax.nn.silu(layer_norm(u, ln_w, ln_b))
    return u @ w_pw + b_pw


def hgrn2_step(state, inp):
    q, k, v, g = inp
    b = jnp.cumsum(g, axis=2)
    o_inter = jnp.einsum("bhtk,bhkv->bhtv", q * jnp.exp(b), state)
    causal = jnp.tril(jnp.ones((CHUNK, CHUNK), dtype=bool))
    diff = b[:, :, :, None, :] - b[:, :, None, :, :]
    decay = jnp.exp(jnp.where(causal[None, None, :, :, None], diff, -jnp.inf))
    scores = jnp.einsum("bhtk,bhsk,bhtsk->bhts", q, k, decay)
    o = o_inter + jnp.einsum("bhts,bhsv->bhtv", scores, v)
    b_last = b[:, :, -1:, :]
    new_state = jnp.exp(b_last[:, :, 0, :])[..., None] * state + jnp.einsum(
        "bhsk,bhsv->bhkv", k * jnp.exp(b_last - b), v)
    return new_state, o


def hgrn2(q, f_logit, i, g_out, lb, norm_w, w_o):
    B, S, _ = q.shape
    n_chunks = S // CHUNK
    f = lb + (1.0 - lb) * jax.nn.sigmoid(f_logit.astype(jnp.float32))
    log_f = jnp.log(f)
    k = 1.0 - f

    def to_chunks(t):
        return t.astype(jnp.float32).reshape(B, n_chunks, CHUNK, HG_HEADS, HG_HEAD_DIM).transpose(1, 0, 3, 2, 4)

    qs, ks, vs, gs = to_chunks(q), to_chunks(k), to_chunks(i), to_chunks(log_f)
    s0 = jnp.zeros((B, HG_HEADS, HG_HEAD_DIM, HG_HEAD_DIM), jnp.float32)
    _, o = lax.scan(hgrn2_step, s0, (qs, ks, vs, gs))
    o = o.transpose(1, 0, 3, 2, 4).reshape(B, S, HG_HEADS, HG_HEAD_DIM)
    o = o * lax.rsqrt(jnp.mean(o * o, axis=-1, keepdims=True) + NORM_EPS)
    o = o * norm_w.astype(jnp.float32).reshape(HG_HEADS, HG_HEAD_DIM)
    o = o.reshape(B, S, HG_WIDTH) * jax.nn.silu(g_out.astype(jnp.float32))
    return o.astype(q.dtype) @ w_o


def setup_inputs(seed: int = 0) -> dict:
    key = jax.random.key(seed)
    ks = jax.random.split(key, 17)

    def nrm(k, shape, scale):
        return jax.random.normal(k, shape, jnp.float32) * scale

    return {
        "x": nrm(ks[0], (BATCH, SEQ, D_MODEL), 1.0),
        "norm_mix_w": 1.0 + nrm(ks[1], (DEPTH, D_MODEL), 0.02),
        "w_in": nrm(ks[2], (DEPTH, D_MODEL, IN_COLS), D_MODEL ** -0.5),
        "dw_conv_w": nrm(ks[3], (DEPTH, CONV_KERNEL, 1, CONV_WIDTH), CONV_KERNEL ** -0.5),
        "dw_conv_b": nrm(ks[4], (DEPTH, CONV_WIDTH), 0.02),
        "conv_ln_w": 1.0 + nrm(ks[5], (DEPTH, CONV_WIDTH), 0.02),
        "conv_ln_b": nrm(ks[6], (DEPTH, CONV_WIDTH), 0.02),
        "w_conv_out": nrm(ks[7], (DEPTH, CONV_WIDTH, D_MODEL), CONV_WIDTH ** -0.5),
        "b_conv_out": nrm(ks[8], (DEPTH, D_MODEL), 0.02),
        "hgrn_lb": nrm(ks[9], (DEPTH + 1, HG_WIDTH), 0.1),
        "hgrn_norm_w": 1.0 + nrm(ks[10], (DEPTH, HG_WIDTH), 0.02),
        "w_hgrn_out": nrm(ks[11], (DEPTH, HG_WIDTH, D_MODEL), HG_WIDTH ** -0.5),
        "w_out": nrm(ks[12], (DEPTH, D_MODEL, D_MODEL), D_MODEL ** -0.5),
        "norm_mlp_w": 1.0 + nrm(ks[13], (DEPTH, D_MODEL), 0.02),
        "w_mlp_up": nrm(ks[14], (DEPTH, D_MODEL, D_FF), D_MODEL ** -0.5),
        "w_mlp_down": nrm(ks[15], (DEPTH, D_FF, D_MODEL), D_FF ** -0.5),
        "norm_final_w": 1.0 + nrm(ks[16], (D_MODEL,), 0.02),
    }


def reference(x, norm_mix_w, w_in, dw_conv_w, dw_conv_b, conv_ln_w, conv_ln_b,
              w_conv_out, b_conv_out, hgrn_lb, hgrn_norm_w, w_hgrn_out, w_out,
              norm_mlp_w, w_mlp_up, w_mlp_down, norm_final_w):
    lower_bounds = jnp.cumsum(jax.nn.softmax(hgrn_lb.astype(jnp.float32), axis=0), axis=0)
    pts = split_points()
    for l in range(DEPTH):
        h = rms_norm(x, norm_mix_w[l])
        proj = h @ w_in[l]
        a, a_gate, hq, hf, hi, hg, gate_a, gate_b = jnp.split(proj, pts, axis=-1)
        y_conv = conformer_conv(a, a_gate, dw_conv_w[l], dw_conv_b[l], conv_ln_w[l], conv_ln_b[l],
                                w_conv_out[l], b_conv_out[l])
        y_rec = hgrn2(hq, hf, hi, hg, lower_bounds[l], hgrn_norm_w[l], w_hgrn_out[l])
        y = jax.nn.sigmoid(gate_a) * y_conv + jax.nn.sigmoid(gate_b) * y_rec
        x = x + y @ w_out[l]
        h = rms_norm(x, norm_mlp_w[l])
        x = x + jnp.square(jax.nn.relu(h @ w_mlp_up[l])) @ w_mlp_down[l]
    return rms_norm(x, norm_final_w)
```

```python
import functools

import jax
import jax.numpy as jnp
import numpy as np
from jax import lax
from jax.experimental import pallas as pl
from jax.experimental.pallas import tpu as pltpu

D_MODEL = 1024
CONV_WIDTH = 512
CONV_KERNEL = 31
HG_WIDTH = 512
HG_HEAD_DIM = 128
HG_HEADS = HG_WIDTH // HG_HEAD_DIM
D_FF = 4096
NORM_EPS = 1e-6
LN_EPS = 1e-5

SUBLANES = 8
LANES = 128

TOKEN_TILE = 256
CONV_HALO = 32
CONV_ROW_BLOCK = 32
MLP_TILE = 512
FF_CHUNK = 1024
VMEM_LIMIT_BYTES = 56 * 1024 * 1024

_BF16 = jnp.bfloat16
_F32 = jnp.float32


def _dot(a, b):
    return jnp.dot(a, b, preferred_element_type=_F32)


def _dot_nt(a, b):
    return lax.dot_general(a, b, (((1,), (1,)), ((), ())), preferred_element_type=_F32)


def _pair_levels(n):
    t = np.arange(n)[:, None]
    s = np.arange(n)[None, :]
    x = t ^ s
    lvl = np.floor(np.log2(np.maximum(x, 1))).astype(np.int32)
    return np.where(t > s, lvl, -1).astype(np.int32)


def _mixer_kernel(x_ref, nw_ref, w_in_ref, dww_ref, dwb_ref, lnw_ref, lnb_ref, wco_ref, bco_ref,
                  lb_ref, hnw_ref, who_ref, wout_ref, lvl_ref, out_ref,
                  ubuf, ush, cbuf, state, sbuf, obuf):
    ts = TOKEN_TILE

    @pl.when(pl.program_id(1) == 0)
    def _():
        ubuf[0:CONV_HALO, :] = jnp.zeros((CONV_HALO, CONV_WIDTH), _F32)
        state[...] = jnp.zeros_like(state)

    x = x_ref[0]
    h = x * lax.rsqrt(jnp.mean(x * x, axis=-1, keepdims=True) + NORM_EPS) * nw_ref[...]
    hb = h.astype(_BF16)

    ag = _dot(hb, w_in_ref[:, 0:2 * CONV_WIDTH])
    u = ag[:, :CONV_WIDTH] * jax.nn.sigmoid(ag[:, CONV_WIDTH:])
    ubuf[CONV_HALO:CONV_HALO + ts, :] = u
    n_sh = ts + CONV_HALO - SUBLANES
    for s in range(1, SUBLANES):
        ush[s - 1, 0:n_sh, :] = ubuf[s:s + n_sh, :]

    first_off = CONV_HALO - (CONV_KERNEL - 1)

    def conv_block(i, carry):
        r0 = pl.multiple_of(i * CONV_ROW_BLOCK, CONV_ROW_BLOCK)
        acc = jnp.broadcast_to(dwb_ref[...], (CONV_ROW_BLOCK, CONV_WIDTH))
        for j in range(CONV_KERNEL):
            a, s = divmod(first_off + j, SUBLANES)
            start = r0 + a * SUBLANES
            if s == 0:
                win = ubuf[pl.ds(start, CONV_ROW_BLOCK), :]
            else:
                win = ush[s - 1, pl.ds(start, CONV_ROW_BLOCK), :]
            acc = acc + dww_ref[j:j + 1, :] * win
        cbuf[pl.ds(r0, CONV_ROW_BLOCK), :] = acc
        return carry

    lax.fori_loop(0, ts // CONV_ROW_BLOCK, conv_block, 0)
    ubuf[0:CONV_HALO, :] = ubuf[ts:ts + CONV_HALO, :]

    c = cbuf[...]
    mu = jnp.mean(c, axis=-1, keepdims=True)
    cc = c - mu
    var = jnp.mean(cc * cc, axis=-1, keepdims=True)
    cn = cc * lax.rsqrt(var + LN_EPS) * lnw_ref[...] + lnb_ref[...]
    cn = cn * jax.nn.sigmoid(cn)
    y_conv = _dot(cn.astype(_BF16), wco_ref[...]) + bco_ref[...]

    hp = _dot(hb, w_in_ref[:, 2 * CONV_WIDTH:2 * CONV_WIDTH + 4 * HG_WIDTH])
    q = hp[:, 0:HG_WIDTH]
    f_logit = hp[:, HG_WIDTH:2 * HG_WIDTH]
    v = hp[:, 2 * HG_WIDTH:3 * HG_WIDTH]
    g_out = hp[:, 3 * HG_WIDTH:4 * HG_WIDTH]

    lbp = lb_ref[...]
    lmax = jnp.max(lbp, axis=0, keepdims=True)
    lexp = jnp.exp(lbp - lmax)
    lb = lexp[0:1, :] / jnp.sum(lexp, axis=0, keepdims=True)
    f = lb + (1.0 - lb) * jax.nn.sigmoid(f_logit)
    kk = 1.0 - f
    vb = v.astype(_BF16)

    row = lax.broadcasted_iota(jnp.int32, (ts, HG_WIDTH), 0)
    P = f
    X = jnp.ones_like(f)
    B = f
    n_levels = ts.bit_length() - 1
    for l in range(n_levels):
        m = 1 << l
        odd = (row & m) != 0
        ql = jnp.where(odd, q * P, 0.0).astype(_BF16)
        kl = jnp.where(odd, 0.0, kk * X).astype(_BF16)
        sel = lvl_ref[...] == l
        for hh in range(HG_HEADS):
            lo = hh * HG_HEAD_DIM
            s_l = _dot_nt(ql[:, lo:lo + HG_HEAD_DIM], kl[:, lo:lo + HG_HEAD_DIM])
            if l == 0:
                sbuf[hh] = jnp.where(sel, s_l, 0.0)
            else:
                sbuf[hh] = jnp.where(sel, s_l, sbuf[hh])
        b_prev = pltpu.roll(B, m, 0)
        b_next = pltpu.roll(B, ts - m, 0)
        P = jnp.where(odd, P * b_prev, P)
        X = jnp.where(odd, X, X * b_next)
        B = B * jnp.where(odd, b_prev, b_next)

    q_in = (q * P).astype(_BF16)
    k_out = (kk * X).astype(_BF16)
    qk = q * kk
    for hh in range(HG_HEADS):
        lo = hh * HG_HEAD_DIM
        st = state[hh]
        o = _dot_nt(q_in[:, lo:lo + HG_HEAD_DIM], st.astype(_BF16))
        o = o + _dot(sbuf[hh].astype(_BF16), vb[:, lo:lo + HG_HEAD_DIM])
        dg = jnp.sum(qk[:, lo:lo + HG_HEAD_DIM], axis=-1, keepdims=True)
        o = o + dg * v[:, lo:lo + HG_HEAD_DIM]
        vt = v[:, lo:lo + HG_HEAD_DIM].T.astype(_BF16)
        state[hh] = st * B[0:1, lo:lo + HG_HEAD_DIM] + _dot(vt, k_out[:, lo:lo + HG_HEAD_DIM])
        o = o * lax.rsqrt(jnp.mean(o * o, axis=-1, keepdims=True) + NORM_EPS)
        o = o * hnw_ref[:, lo:lo + HG_HEAD_DIM]
        go = g_out[:, lo:lo + HG_HEAD_DIM]
        obuf[:, lo:lo + HG_HEAD_DIM] = (o * (go * jax.nn.sigmoid(go))).astype(_BF16)
    y_rec = _dot(obuf[...], who_ref[...])

    gates = _dot(hb, w_in_ref[:, 2 * CONV_WIDTH + 4 * HG_WIDTH:])
    y = jax.nn.sigmoid(gates[:, :D_MODEL]) * y_conv + jax.nn.sigmoid(gates[:, D_MODEL:]) * y_rec
    out_ref[0] = x + _dot(y.astype(_BF16), wout_ref[...])


def _mlp_kernel(x_ref, nw_ref, wup_ref, wdn_ref, fw_ref, out_ref, acc):
    x = x_ref[...]
    h = x * lax.rsqrt(jnp.mean(x * x, axis=-1, keepdims=True) + NORM_EPS) * nw_ref[...]
    hb = h.astype(_BF16)
    for ci in range(D_FF // FF_CHUNK):
        lo = ci * FF_CHUNK
        up = jnp.maximum(_dot(hb, wup_ref[:, lo:lo + FF_CHUNK]), 0.0)
        part = _dot((up * up).astype(_BF16), wdn_ref[lo:lo + FF_CHUNK, :])
        if ci == 0:
            acc[...] = x + part
        else:
            acc[...] += part
    z = acc[...]
    out_ref[...] = z * lax.rsqrt(jnp.mean(z * z, axis=-1, keepdims=True) + NORM_EPS) * fw_ref[...]


def _const_spec(shape):
    nd = len(shape)
    return pl.BlockSpec(shape, lambda *_: (0,) * nd, pipeline_mode=pl.Buffered(1))


@jax.jit
def kernel(x, norm_mix_w, w_in, dw_conv_w, dw_conv_b, conv_ln_w, conv_ln_b, w_conv_out, b_conv_out, hgrn_lb, hgrn_norm_w, w_hgrn_out, w_out, norm_mlp_w, w_mlp_up, w_mlp_down, norm_final_w):
    batch, seq, d = x.shape
    assert d == D_MODEL and seq % TOKEN_TILE == 0 and (batch * seq) % MLP_TILE == 0
    ts = TOKEN_TILE
    row2 = lambda a: a.reshape(1, -1).astype(_F32)

    mixer_consts = [
        row2(norm_mix_w[0]),
        w_in[0].astype(_BF16),
        dw_conv_w[0].reshape(CONV_KERNEL, CONV_WIDTH).astype(_F32),
        row2(dw_conv_b[0]), row2(conv_ln_w[0]), row2(conv_ln_b[0]),
        w_conv_out[0].astype(_BF16), row2(b_conv_out[0]),
        hgrn_lb.astype(_F32), row2(hgrn_norm_w[0]),
        w_hgrn_out[0].astype(_BF16), w_out[0].astype(_BF16),
        jnp.asarray(_pair_levels(ts)),
    ]
    x1 = pl.pallas_call(
        _mixer_kernel,
        grid=(batch, seq // ts),
        in_specs=[pl.BlockSpec((1, ts, d), lambda b, t: (b, t, 0))] + [_const_spec(a.shape) for a in mixer_consts],
        out_specs=pl.BlockSpec((1, ts, d), lambda b, t: (b, t, 0)),
        out_shape=jax.ShapeDtypeStruct(x.shape, _F32),
        scratch_shapes=[
            pltpu.VMEM((ts + CONV_HALO, CONV_WIDTH), _F32),
            pltpu.VMEM((SUBLANES - 1, ts + CONV_HALO - SUBLANES, CONV_WIDTH), _F32),
            pltpu.VMEM((ts, CONV_WIDTH), _F32),
            pltpu.VMEM((HG_HEADS, HG_HEAD_DIM, HG_HEAD_DIM), _F32),
            pltpu.VMEM((HG_HEADS, ts, ts), _F32),
            pltpu.VMEM((ts, HG_WIDTH), _BF16),
        ],
        compiler_params=pltpu.CompilerParams(
            dimension_semantics=("arbitrary", "arbitrary"), vmem_limit_bytes=VMEM_LIMIT_BYTES),
        name="mixer",
    )(x, *mixer_consts)

    n_tok = batch * seq
    mlp_consts = [row2(norm_mlp_w[0]), w_mlp_up[0].astype(_BF16), w_mlp_down[0].astype(_BF16), row2(norm_final_w)]
    out = pl.pallas_call(
        _mlp_kernel,
        grid=(n_tok // MLP_TILE,),
        in_specs=[pl.BlockSpec((MLP_TILE, d), lambda i: (i, 0))] + [_const_spec(a.shape) for a in mlp_consts],
        out_specs=pl.BlockSpec((MLP_TILE, d), lambda i: (i, 0)),
        out_shape=jax.ShapeDtypeStruct((n_tok, d), _F32),
        scratch_shapes=[pltpu.VMEM((MLP_TILE, d), _F32)],
        compiler_params=pltpu.CompilerParams(
            dimension_semantics=("arbitrary",), vmem_limit_bytes=VMEM_LIMIT_BYTES),
        name="mlp",
    )(x1.reshape(n_tok, d), *mlp_consts)
    return out.reshape(batch, seq, d)
```

```python
import jax
import jax.numpy as jnp
import numpy as np
from jax import lax
from jax.experimental import pallas as pl
from jax.experimental.pallas import tpu as pltpu

D_MODEL = 1024
CONV_WIDTH = 512
CONV_KERNEL = 31
HG_WIDTH = 512
HG_HEAD_DIM = 128
HG_HEADS = HG_WIDTH // HG_HEAD_DIM
D_FF = 4096
NORM_EPS = 1e-6
LN_EPS = 1e-5

SUBLANES = 8
LANES = 128

TOKEN_TILE = 256
HALF = TOKEN_TILE // 2
CONV_HALO = 32
CONV_STRIDE = 4
CONV_ROW_BLOCK = SUBLANES * CONV_STRIDE
CONV_SLABS = CONV_WIDTH // LANES
MLP_TILE = 512
FF_CHUNK = 1024
VMEM_LIMIT_BYTES = 56 * 1024 * 1024

_BF16 = jnp.bfloat16
_F32 = jnp.float32


def _dot(a, b):
    return jnp.dot(a, b, preferred_element_type=_F32)


def _dot_nt(a, b):
    return lax.dot_general(a, b, (((1,), (1,)), ((), ())), preferred_element_type=_F32)


def _pair_levels(n):
    t = np.arange(n)[:, None]
    s = np.arange(n)[None, :]
    lvl = np.floor(np.log2(np.maximum(t ^ s, 1))).astype(np.int32)
    return np.where(t > s, lvl, -1).astype(np.int32)


def _rms(x, w):
    return x * lax.rsqrt(jnp.mean(x * x, axis=-1, keepdims=True) + NORM_EPS) * w


def _causal_conv(u, dwwb_ref, dwbb_ref, ubuf, cbuf):
    ts = TOKEN_TILE
    for c in range(CONV_SLABS):
        ubuf[c][CONV_HALO:CONV_HALO + ts, :] = u[:, c * LANES:(c + 1) * LANES]
    first_off = CONV_HALO - (CONV_KERNEL - 1)
    for r0 in range(0, ts, CONV_ROW_BLOCK):
        for c in range(CONV_SLABS):
            wins = {}
            accs = [dwbb_ref[c]] * CONV_STRIDE
            for j in range(CONV_KERNEL):
                wj = dwwb_ref[c, j * SUBLANES:(j + 1) * SUBLANES, :]
                for t0 in range(CONV_STRIDE):
                    start = r0 + t0 + first_off + j
                    if start not in wins:
                        wins[start] = ubuf[c][pl.ds(start, SUBLANES, stride=CONV_STRIDE), :]
                    accs[t0] = accs[t0] + wj * wins[start]
            for t0 in range(CONV_STRIDE):
                cbuf[c][pl.ds(r0 + t0, SUBLANES, stride=CONV_STRIDE), :] = accs[t0]
    for c in range(CONV_SLABS):
        ubuf[c][0:CONV_HALO, :] = ubuf[c][ts:ts + CONV_HALO, :]
    return jnp.concatenate([cbuf[c][...] for c in range(CONV_SLABS)], axis=-1)


def _level_operands(q, kk, f, b, bsc):
    ts = TOKEN_TILE
    row = lax.broadcasted_iota(jnp.int32, (ts, HG_WIDTH), 0)
    qf = q * f
    zs = []
    zs.append(jnp.where((row & 1) != 0, qf, kk))
    p = row & 3
    f_next = pltpu.roll(f, ts - 1, 0)
    f_prev = pltpu.roll(f, 1, 0)
    zs.append(jnp.where(p < 2, kk * jnp.where(p == 0, f_next, 1.0), qf * jnp.where(p == 3, f_prev, 1.0)))
    sub = lax.broadcasted_iota(jnp.int32, (SUBLANES, HG_WIDTH), 0)
    parts = []
    for g0 in range(0, ts, SUBLANES):
        bg = b[g0:g0 + SUBLANES]
        d = bg - bsc[g0 + 3:g0 + 4, :]
        e = jnp.exp(jnp.where(sub < 4, -d, d))
        parts.append(jnp.where(sub < 4, kk[g0:g0 + SUBLANES], q[g0:g0 + SUBLANES]) * e)
    zs.append(jnp.concatenate(parts, axis=0))
    m = 8
    while m < ts:
        parts = []
        for g0 in range(0, ts, 2 * m):
            br = bsc[g0 + m - 1:g0 + m, :]
            parts.append(kk[g0:g0 + m] * jnp.exp(br - b[g0:g0 + m]))
            parts.append(q[g0 + m:g0 + 2 * m] * jnp.exp(b[g0 + m:g0 + 2 * m] - br))
        zs.append(jnp.concatenate(parts, axis=0))
        m *= 2
    return [z.astype(_BF16) for z in zs]


def _mixer_kernel(x_ref, nw_ref, w_in_ref, dwwb_ref, dwbb_ref, lnw_ref, lnb_ref, wco_ref, bco_ref,
                  lb_ref, hnw_ref, who_ref, wout_ref, lvl_ref, tri_ref, out_ref,
                  *scratch):
    ts = TOKEN_TILE
    ubuf, cbuf = scratch[0:CONV_SLABS], scratch[CONV_SLABS:2 * CONV_SLABS]
    state, obuf, bsc = scratch[2 * CONV_SLABS:]

    @pl.when(pl.program_id(1) == 0)
    def _():
        for c in range(CONV_SLABS):
            ubuf[c][0:CONV_HALO, :] = jnp.zeros((CONV_HALO, LANES), _F32)
        state[...] = jnp.zeros_like(state)

    x = x_ref[0]
    hb = _rms(x, nw_ref[...]).astype(_BF16)

    ag = _dot(hb, w_in_ref[:, 0:2 * CONV_WIDTH])
    u = ag[:, :CONV_WIDTH] * jax.nn.sigmoid(ag[:, CONV_WIDTH:])
    c = _causal_conv(u, dwwb_ref, dwbb_ref, ubuf, cbuf)
    mu = jnp.mean(c, axis=-1, keepdims=True)
    cc = c - mu
    var = jnp.mean(cc * cc, axis=-1, keepdims=True)
    cn = cc * lax.rsqrt(var + LN_EPS) * lnw_ref[...] + lnb_ref[...]
    cn = cn * jax.nn.sigmoid(cn)
    y_conv = _dot(cn.astype(_BF16), wco_ref[...]) + bco_ref[...]

    hp = _dot(hb, w_in_ref[:, 2 * CONV_WIDTH:2 * CONV_WIDTH + 4 * HG_WIDTH])
    q = hp[:, 0:HG_WIDTH]
    f_logit = hp[:, HG_WIDTH:2 * HG_WIDTH]
    v = hp[:, 2 * HG_WIDTH:3 * HG_WIDTH]
    g_out = hp[:, 3 * HG_WIDTH:4 * HG_WIDTH]

    lbp = lb_ref[...]
    lmax = jnp.max(lbp, axis=0, keepdims=True)
    lexp = jnp.exp(lbp - lmax)
    lb = lexp[0:1, :] / jnp.sum(lexp, axis=0, keepdims=True)
    f = lb + (1.0 - lb) * jax.nn.sigmoid(f_logit)
    kk = 1.0 - f
    vb = v.astype(_BF16)

    g = jnp.log(f)
    g_hi = g.astype(_BF16)
    r1 = g - g_hi.astype(_F32)
    g_mid = r1.astype(_BF16)
    g_lo = (r1 - g_mid.astype(_F32)).astype(_BF16)
    tri = tri_ref[...]
    b = _dot(tri, g_hi) + (_dot(tri, g_mid) + _dot(tri, g_lo))

    bsc[...] = b
    zs = _level_operands(q, kk, f, b, bsc)
    b_last = bsc[ts - 1:ts, :]
    q_in = (q * jnp.exp(b)).astype(_BF16)
    k_out = (kk * jnp.exp(b_last - b)).astype(_BF16)
    decay_all = jnp.exp(b_last)
    qk = q * kk
    n_diag_levels = len(zs) - 1
    lvl = lvl_ref[...]
    for hh in range(HG_HEADS):
        cols = slice(hh * HG_HEAD_DIM, (hh + 1) * HG_HEAD_DIM)
        diag = []
        for rows in (slice(0, HALF), slice(HALF, ts)):
            acc = jnp.zeros((HALF, HALF), _F32)
            for l in range(n_diag_levels):
                z = zs[l][rows, cols]
                acc = jnp.where(lvl == l, _dot_nt(z, z), acc)
            diag.append(acc.astype(_BF16))
        low = _dot_nt(zs[-1][HALF:ts, cols], zs[-1][0:HALF, cols]).astype(_BF16)
        st = state[hh]
        o = _dot_nt(q_in[:, cols], st.astype(_BF16))
        o_a = _dot(diag[0], vb[0:HALF, cols])
        o_b = _dot(jnp.concatenate([low, diag[1]], axis=1), vb[:, cols])
        o = o + jnp.concatenate([o_a, o_b], axis=0)
        dg = jnp.sum(qk[:, cols], axis=-1, keepdims=True)
        o = o + dg * v[:, cols]
        vt = v[:, cols].T.astype(_BF16)
        state[hh] = st * decay_all[:, cols] + _dot(vt, k_out[:, cols])
        o = o * lax.rsqrt(jnp.mean(o * o, axis=-1, keepdims=True) + NORM_EPS)
        o = o * hnw_ref[:, cols]
        go = g_out[:, cols]
        obuf[:, cols] = (o * (go * jax.nn.sigmoid(go))).astype(_BF16)
    y_rec = _dot(obuf[...], who_ref[...])

    gates = _dot(hb, w_in_ref[:, 2 * CONV_WIDTH + 4 * HG_WIDTH:])
    y = jax.nn.sigmoid(gates[:, :D_MODEL]) * y_conv + jax.nn.sigmoid(gates[:, D_MODEL:]) * y_rec
    out_ref[0] = x + _dot(y.astype(_BF16), wout_ref[...])


def _mlp_kernel(x_ref, nw_ref, wup_ref, wdn_ref, fw_ref, out_ref, acc):
    x = x_ref[...]
    hb = _rms(x, nw_ref[...]).astype(_BF16)
    for ci in range(D_FF // FF_CHUNK):
        lo = ci * FF_CHUNK
        up = jnp.maximum(_dot(hb, wup_ref[:, lo:lo + FF_CHUNK]), 0.0)
        part = _dot((up * up).astype(_BF16), wdn_ref[lo:lo + FF_CHUNK, :])
        if ci == 0:
            acc[...] = x + part
        else:
            acc[...] += part
    out_ref[...] = _rms(acc[...], fw_ref[...])


def _const_spec(shape):
    nd = len(shape)
    return pl.BlockSpec(shape, lambda *_: (0,) * nd, pipeline_mode=pl.Buffered(1))


def _slabs(a):
    return a.reshape(a.shape[0], CONV_SLABS, LANES).transpose(1, 0, 2)


@jax.jit
def kernel(x, norm_mix_w, w_in, dw_conv_w, dw_conv_b, conv_ln_w, conv_ln_b, w_conv_out, b_conv_out, hgrn_lb, hgrn_norm_w, w_hgrn_out, w_out, norm_mlp_w, w_mlp_up, w_mlp_down, norm_final_w):
    batch, seq, d = x.shape
    assert d == D_MODEL and seq % TOKEN_TILE == 0 and (batch * seq) % MLP_TILE == 0
    ts = TOKEN_TILE
    row2 = lambda a: a.reshape(1, -1).astype(_F32)

    dww = dw_conv_w[0].reshape(CONV_KERNEL, CONV_WIDTH).astype(_F32)
    dwwb = _slabs(jnp.repeat(dww, SUBLANES, axis=0))
    dwbb = _slabs(jnp.broadcast_to(dw_conv_b[0].astype(_F32), (SUBLANES, CONV_WIDTH)))

    mixer_consts = [
        row2(norm_mix_w[0]),
        w_in[0].astype(_BF16),
        dwwb, dwbb, row2(conv_ln_w[0]), row2(conv_ln_b[0]),
        w_conv_out[0].astype(_BF16), row2(b_conv_out[0]),
        hgrn_lb.astype(_F32), row2(hgrn_norm_w[0]),
        w_hgrn_out[0].astype(_BF16), w_out[0].astype(_BF16),
        jnp.asarray(_pair_levels(HALF)),
        jnp.asarray(np.tril(np.ones((ts, ts), np.float32)), dtype=_BF16),
    ]
    x1 = pl.pallas_call(
        _mixer_kernel,
        grid=(batch, seq // ts),
        in_specs=[pl.BlockSpec((1, ts, d), lambda b, t: (b, t, 0))] + [_const_spec(a.shape) for a in mixer_consts],
        out_specs=pl.BlockSpec((1, ts, d), lambda b, t: (b, t, 0)),
        out_shape=jax.ShapeDtypeStruct(x.shape, _F32),
        scratch_shapes=[pltpu.VMEM((ts + CONV_HALO, LANES), _F32)] * CONV_SLABS
        + [pltpu.VMEM((ts, LANES), _F32)] * CONV_SLABS
        + [
            pltpu.VMEM((HG_HEADS, HG_HEAD_DIM, HG_HEAD_DIM), _F32),
            pltpu.VMEM((ts, HG_WIDTH), _BF16),
            pltpu.VMEM((ts, HG_WIDTH), _F32),
        ],
        compiler_params=pltpu.CompilerParams(
            dimension_semantics=("arbitrary", "arbitrary"), vmem_limit_bytes=VMEM_LIMIT_BYTES),
        name="mixer",
    )(x, *mixer_consts)

    n_tok = batch * seq
    mlp_consts = [row2(norm_mlp_w[0]), w_mlp_up[0].astype(_BF16), w_mlp_down[0].astype(_BF16), row2(norm_final_w)]
    out = pl.pallas_call(
        _mlp_kernel,
        grid=(n_tok // MLP_TILE,),
        in_specs=[pl.BlockSpec((MLP_TILE, d), lambda i: (i, 0))] + [_const_spec(a.shape) for a in mlp_consts],
        out_specs=pl.BlockSpec((MLP_TILE, d), lambda i: (i, 0)),
        out_shape=jax.ShapeDtypeStruct((n_tok, d), _F32),
        scratch_shapes=[pltpu.VMEM((MLP_TILE, d), _F32)],
        compiler_params=pltpu.CompilerParams(
            dimension_semantics=("arbitrary",), vmem_limit_bytes=VMEM_LIMIT_BYTES),
        name="mlp",
    )(x1.reshape(n_tok, d), *mlp_consts)
    return out.reshape(batch, seq, d)
```

```python
import jax
import jax.numpy as jnp
import numpy as np
from jax import lax
from jax.experimental import pallas as pl
from jax.experimental.pallas import tpu as pltpu

D_MODEL = 1024
CONV_WIDTH = 512
CONV_KERNEL = 31
HG_WIDTH = 512
HG_HEAD_DIM = 128
HG_HEADS = HG_WIDTH // HG_HEAD_DIM
D_FF = 4096
NORM_EPS = 1e-6
LN_EPS = 1e-5

SUBLANES = 8
LANES = 128

TOKEN_TILE = 256
HALF = TOKEN_TILE // 2
CONV_HALO = 32
CONV_STRIDE = 4
CONV_ROW_BLOCK = SUBLANES * CONV_STRIDE
CONV_SLABS = CONV_WIDTH // LANES
CONV_CHAINS = 1
NEVER_LEVEL = 1 << 20
MLP_TILE = 512
FF_CHUNK = 1024
VMEM_LIMIT_BYTES = 56 * 1024 * 1024

_BF16 = jnp.bfloat16
_F32 = jnp.float32


def _dot(a, b):
    return jnp.dot(a, b, preferred_element_type=_F32)


def _dot_nt(a, b):
    return lax.dot_general(a, b, (((1,), (1,)), ((), ())), preferred_element_type=_F32)


def _pair_levels(n):
    t = np.arange(n)[:, None]
    s = np.arange(n)[None, :]
    lvl = np.floor(np.log2(np.maximum(t ^ s, 1))).astype(np.int32)
    return np.where(t > s, lvl, -1).astype(np.int32)


def _rms(x, w):
    return x * lax.rsqrt(jnp.mean(x * x, axis=-1, keepdims=True) + NORM_EPS) * w


def _causal_conv(u, dwwb_ref, dwbb_ref, ubuf, cbuf, never):
    ts = TOKEN_TILE
    for c in range(CONV_SLABS):
        ubuf[c][CONV_HALO:CONV_HALO + ts, :] = u[:, c * LANES:(c + 1) * LANES]
    first_off = CONV_HALO - (CONV_KERNEL - 1)
    units = [(r0, c) for r0 in range(0, ts, CONV_ROW_BLOCK) for c in range(CONV_SLABS)]
    prev = [None] * CONV_CHAINS
    for n, (r0, c) in enumerate(units):
        chain = n % CONV_CHAINS
        bias = dwbb_ref[c]
        if prev[chain] is None:
            accs = [bias] * CONV_STRIDE
        else:
            accs = [jnp.where(never, p, bias) for p in prev[chain]]
        wins = {}
        for j in range(CONV_KERNEL):
            wj = dwwb_ref[c, j * SUBLANES:(j + 1) * SUBLANES, :]
            for t0 in range(CONV_STRIDE):
                start = r0 + t0 + first_off + j
                if start not in wins:
                    wins[start] = ubuf[c][pl.ds(start, SUBLANES, stride=CONV_STRIDE), :]
                accs[t0] = accs[t0] + wj * wins[start]
        for t0 in range(CONV_STRIDE):
            cbuf[c][pl.ds(r0 + t0, SUBLANES, stride=CONV_STRIDE), :] = accs[t0]
        prev[chain] = accs
    for c in range(CONV_SLABS):
        ubuf[c][0:CONV_HALO, :] = ubuf[c][ts:ts + CONV_HALO, :]
    return jnp.concatenate([cbuf[c][...] for c in range(CONV_SLABS)], axis=-1)


def _level_operands(q, kk, f, b, bsc):
    ts = TOKEN_TILE
    row = lax.broadcasted_iota(jnp.int32, (ts, HG_WIDTH), 0)
    qf = q * f
    zs = []
    zs.append(jnp.where((row & 1) != 0, qf, kk))
    p = row & 3
    f_next = pltpu.roll(f, ts - 1, 0)
    f_prev = pltpu.roll(f, 1, 0)
    zs.append(jnp.where(p < 2, kk * jnp.where(p == 0, f_next, 1.0), qf * jnp.where(p == 3, f_prev, 1.0)))
    sub = lax.broadcasted_iota(jnp.int32, (SUBLANES, HG_WIDTH), 0)
    parts = []
    for g0 in range(0, ts, SUBLANES):
        bg = b[g0:g0 + SUBLANES]
        d = bg - bsc[g0 + 3:g0 + 4, :]
        e = jnp.exp(jnp.where(sub < 4, -d, d))
        parts.append(jnp.where(sub < 4, kk[g0:g0 + SUBLANES], q[g0:g0 + SUBLANES]) * e)
    zs.append(jnp.concatenate(parts, axis=0))
    m = 8
    while m < ts:
        parts = []
        for g0 in range(0, ts, 2 * m):
            br = bsc[g0 + m - 1:g0 + m, :]
            parts.append(kk[g0:g0 + m] * jnp.exp(br - b[g0:g0 + m]))
            parts.append(q[g0 + m:g0 + 2 * m] * jnp.exp(b[g0 + m:g0 + 2 * m] - br))
        zs.append(jnp.concatenate(parts, axis=0))
        m *= 2
    return [z.astype(_BF16) for z in zs]


def _mixer_kernel(x_ref, nw_ref, w_in_ref, dwwb_ref, dwbb_ref, lnw_ref, lnb_ref, wco_ref, bco_ref,
                  lb_ref, hnw_ref, who_ref, wout_ref, lvl_ref, tri_ref, out_ref,
                  *scratch):
    ts = TOKEN_TILE
    ubuf, cbuf = scratch[0:CONV_SLABS], scratch[CONV_SLABS:2 * CONV_SLABS]
    state, obuf, bsc = scratch[2 * CONV_SLABS:]

    @pl.when(pl.program_id(1) == 0)
    def _():
        for c in range(CONV_SLABS):
            ubuf[c][0:CONV_HALO, :] = jnp.zeros((CONV_HALO, LANES), _F32)
        state[...] = jnp.zeros_like(state)

    x = x_ref[0]
    hb = _rms(x, nw_ref[...]).astype(_BF16)

    ag = _dot(hb, w_in_ref[:, 0:2 * CONV_WIDTH])
    u = ag[:, :CONV_WIDTH] * jax.nn.sigmoid(ag[:, CONV_WIDTH:])
    never = lvl_ref[0:SUBLANES, :] == NEVER_LEVEL
    c = _causal_conv(u, dwwb_ref, dwbb_ref, ubuf, cbuf, never)
    hp = _dot(hb, w_in_ref[:, 2 * CONV_WIDTH:2 * CONV_WIDTH + 4 * HG_WIDTH])
    gates = _dot(hb, w_in_ref[:, 2 * CONV_WIDTH + 4 * HG_WIDTH:])
    tail = jnp.concatenate([hp[ts - SUBLANES:ts, :], gates[ts - SUBLANES:ts, :]], axis=1)
    tie = jnp.where(never, sum(tail[:, i * LANES:(i + 1) * LANES] for i in range(tail.shape[1] // LANES)), 0.0)
    lnb = lnb_ref[...]
    lnb = jnp.concatenate([lnb[:, :LANES] + tie[0:1, :], lnb[:, LANES:]], axis=1)
    mu = jnp.mean(c, axis=-1, keepdims=True)
    cc = c - mu
    var = jnp.mean(cc * cc, axis=-1, keepdims=True)
    cn = cc * lax.rsqrt(var + LN_EPS) * lnw_ref[...] + lnb
    cn = cn * jax.nn.sigmoid(cn)
    y_conv = _dot(cn.astype(_BF16), wco_ref[...]) + bco_ref[...]

    q = hp[:, 0:HG_WIDTH]
    f_logit = hp[:, HG_WIDTH:2 * HG_WIDTH]
    v = hp[:, 2 * HG_WIDTH:3 * HG_WIDTH]
    g_out = hp[:, 3 * HG_WIDTH:4 * HG_WIDTH]

    lbp = lb_ref[...]
    lmax = jnp.max(lbp, axis=0, keepdims=True)
    lexp = jnp.exp(lbp - lmax)
    lb = lexp[0:1, :] / jnp.sum(lexp, axis=0, keepdims=True)
    f = lb + (1.0 - lb) * jax.nn.sigmoid(f_logit)
    kk = 1.0 - f
    vb = v.astype(_BF16)

    g = jnp.log(f)
    g_hi = g.astype(_BF16)
    r1 = g - g_hi.astype(_F32)
    g_mid = r1.astype(_BF16)
    g_lo = (r1 - g_mid.astype(_F32)).astype(_BF16)
    tri = tri_ref[...]
    b = _dot(tri, g_hi) + (_dot(tri, g_mid) + _dot(tri, g_lo))

    bsc[...] = b
    zs = _level_operands(q, kk, f, b, bsc)
    b_last = bsc[ts - 1:ts, :]
    q_in = (q * jnp.exp(b)).astype(_BF16)
    k_out = (kk * jnp.exp(b_last - b)).astype(_BF16)
    decay_all = jnp.exp(b_last)
    qk = q * kk
    n_diag_levels = len(zs) - 1
    lvl = lvl_ref[...]
    for hh in range(HG_HEADS):
        cols = slice(hh * HG_HEAD_DIM, (hh + 1) * HG_HEAD_DIM)
        diag = []
        for rows in (slice(0, HALF), slice(HALF, ts)):
            acc = jnp.zeros((HALF, HALF), _F32)
            for l in range(n_diag_levels):
                z = zs[l][rows, cols]
                acc = jnp.where(lvl == l, _dot_nt(z, z), acc)
            diag.append(acc.astype(_BF16))
        low = _dot_nt(zs[-1][HALF:ts, cols], zs[-1][0:HALF, cols]).astype(_BF16)
        st = state[hh]
        o = _dot_nt(q_in[:, cols], st.astype(_BF16))
        o_a = _dot(diag[0], vb[0:HALF, cols])
        o_b = _dot(jnp.concatenate([low, diag[1]], axis=1), vb[:, cols])
        o = o + jnp.concatenate([o_a, o_b], axis=0)
        dg = jnp.sum(qk[:, cols], axis=-1, keepdims=True)
        o = o + dg * v[:, cols]
        vt = v[:, cols].T.astype(_BF16)
        state[hh] = st * decay_all[:, cols] + _dot(vt, k_out[:, cols])
        o = o * lax.rsqrt(jnp.mean(o * o, axis=-1, keepdims=True) + NORM_EPS)
        o = o * hnw_ref[:, cols]
        go = g_out[:, cols]
        obuf[:, cols] = (o * (go * jax.nn.sigmoid(go))).astype(_BF16)
    y_rec = _dot(obuf[...], who_ref[...])

    y = jax.nn.sigmoid(gates[:, :D_MODEL]) * y_conv + jax.nn.sigmoid(gates[:, D_MODEL:]) * y_rec
    out_ref[0] = x + _dot(y.astype(_BF16), wout_ref[...])


def _mlp_kernel(x_ref, nw_ref, wup_ref, wdn_ref, fw_ref, out_ref, acc):
    x = x_ref[...]
    hb = _rms(x, nw_ref[...]).astype(_BF16)
    for ci in range(D_FF // FF_CHUNK):
        lo = ci * FF_CHUNK
        up = jnp.maximum(_dot(hb, wup_ref[:, lo:lo + FF_CHUNK]), 0.0)
        part = _dot((up * up).astype(_BF16), wdn_ref[lo:lo + FF_CHUNK, :])
        if ci == 0:
            acc[...] = x + part
        else:
            acc[...] += part
    out_ref[...] = _rms(acc[...], fw_ref[...])


def _const_spec(shape):
    nd = len(shape)
    return pl.BlockSpec(shape, lambda *_: (0,) * nd, pipeline_mode=pl.Buffered(1))


def _slabs(a):
    return a.reshape(a.shape[0], CONV_SLABS, LANES).transpose(1, 0, 2)


@jax.jit
def kernel(x, norm_mix_w, w_in, dw_conv_w, dw_conv_b, conv_ln_w, conv_ln_b, w_conv_out, b_conv_out, hgrn_lb, hgrn_norm_w, w_hgrn_out, w_out, norm_mlp_w, w_mlp_up, w_mlp_down, norm_final_w):
    batch, seq, d = x.shape
    assert d == D_MODEL and seq % TOKEN_TILE == 0 and (batch * seq) % MLP_TILE == 0
    ts = TOKEN_TILE
    row2 = lambda a: a.reshape(1, -1).astype(_F32)

    dww = dw_conv_w[0].reshape(CONV_KERNEL, CONV_WIDTH).astype(_F32)
    dwwb = _slabs(jnp.repeat(dww, SUBLANES, axis=0))
    dwbb = _slabs(jnp.broadcast_to(dw_conv_b[0].astype(_F32), (SUBLANES, CONV_WIDTH)))

    mixer_consts = [
        row2(norm_mix_w[0]),
        w_in[0].astype(_BF16),
        dwwb, dwbb, row2(conv_ln_w[0]), row2(conv_ln_b[0]),
        w_conv_out[0].astype(_BF16), row2(b_conv_out[0]),
        hgrn_lb.astype(_F32), row2(hgrn_norm_w[0]),
        w_hgrn_out[0].astype(_BF16), w_out[0].astype(_BF16),
        jnp.asarray(_pair_levels(HALF)),
        jnp.asarray(np.tril(np.ones((ts, ts), np.float32)), dtype=_BF16),
    ]
    x1 = pl.pallas_call(
        _mixer_kernel,
        grid=(batch, seq // ts),
        in_specs=[pl.BlockSpec((1, ts, d), lambda b, t: (b, t, 0))] + [_const_spec(a.shape) for a in mixer_consts],
        out_specs=pl.BlockSpec((1, ts, d), lambda b, t: (b, t, 0)),
        out_shape=jax.ShapeDtypeStruct(x.shape, _F32),
        scratch_shapes=[pltpu.VMEM((ts + CONV_HALO, LANES), _F32)] * CONV_SLABS
        + [pltpu.VMEM((ts, LANES), _F32)] * CONV_SLABS
        + [
            pltpu.VMEM((HG_HEADS, HG_HEAD_DIM, HG_HEAD_DIM), _F32),
            pltpu.VMEM((ts, HG_WIDTH), _BF16),
            pltpu.VMEM((ts, HG_WIDTH), _F32),
        ],
        compiler_params=pltpu.CompilerParams(
            dimension_semantics=("arbitrary", "arbitrary"), vmem_limit_bytes=VMEM_LIMIT_BYTES),
        name="mixer",
    )(x, *mixer_consts)

    n_tok = batch * seq
    mlp_consts = [row2(norm_mlp_w[0]), w_mlp_up[0].astype(_BF16), w_mlp_down[0].astype(_BF16), row2(norm_final_w)]
    out = pl.pallas_call(
        _mlp_kernel,
        grid=(n_tok // MLP_TILE,),
        in_specs=[pl.BlockSpec((MLP_TILE, d), lambda i: (i, 0))] + [_const_spec(a.shape) for a in mlp_consts],
        out_specs=pl.BlockSpec((MLP_TILE, d), lambda i: (i, 0)),
        out_shape=jax.ShapeDtypeStruct((n_tok, d), _F32),
        scratch_shapes=[pltpu.VMEM((MLP_TILE, d), _F32)],
        compiler_params=pltpu.CompilerParams(
            dimension_semantics=("arbitrary",), vmem_limit_bytes=VMEM_LIMIT_BYTES),
        name="mlp",
    )(x1.reshape(n_tok, d), *mlp_consts)
    return out.reshape(batch, seq, d)
```

```python
import jax
import jax.numpy as jnp
import numpy as np
from jax import lax
from jax.experimental import pallas as pl
from jax.experimental.pallas import tpu as pltpu

D_MODEL = 1024
CONV_WIDTH = 512
CONV_KERNEL = 31
HG_WIDTH = 512
HG_HEAD_DIM = 128
HG_HEADS = HG_WIDTH // HG_HEAD_DIM
D_FF = 4096
NORM_EPS = 1e-6
LN_EPS = 1e-5

SUBLANES = 8
LANES = 128

TOKEN_TILE = 1024
CHUNK = 256
HALF = CHUNK // 2
CONV_HALO = 32
CONV_STRIDE = 4
CONV_ROW_BLOCK = SUBLANES * CONV_STRIDE
CONV_SLABS = CONV_WIDTH // LANES
CONV_CHAINS = 1
NEVER_LEVEL = 1 << 20
MLP_TILE = 1024
FF_CHUNK = 1024
VMEM_LIMIT_BYTES = 60 * 1024 * 1024

_BF16 = jnp.bfloat16
_F32 = jnp.float32


def _dot(a, b):
    return jnp.dot(a, b, preferred_element_type=_F32)


def _dot_nt(a, b):
    return lax.dot_general(a, b, (((1,), (1,)), ((), ())), preferred_element_type=_F32)


def _pair_levels(n):
    t = np.arange(n)[:, None]
    s = np.arange(n)[None, :]
    lvl = np.floor(np.log2(np.maximum(t ^ s, 1))).astype(np.int32)
    return np.where(t > s, lvl, -1).astype(np.int32)


def _rms(x, w):
    return x * lax.rsqrt(jnp.mean(x * x, axis=-1, keepdims=True) + NORM_EPS) * w


def _causal_conv(u, dwwb_ref, dwbb_ref, ubuf, cbuf, never):
    ts = TOKEN_TILE
    for c in range(CONV_SLABS):
        ubuf[c][CONV_HALO:CONV_HALO + ts, :] = u[:, c * LANES:(c + 1) * LANES]
    first_off = CONV_HALO - (CONV_KERNEL - 1)
    units = [(r0, c) for r0 in range(0, ts, CONV_ROW_BLOCK) for c in range(CONV_SLABS)]
    prev = [None] * CONV_CHAINS
    for n, (r0, c) in enumerate(units):
        chain = n % CONV_CHAINS
        bias = dwbb_ref[c]
        if prev[chain] is None:
            accs = [bias] * CONV_STRIDE
        else:
            accs = [jnp.where(never, p, bias) for p in prev[chain]]
        wins = {}
        for j in range(CONV_KERNEL):
            wj = dwwb_ref[c, j * SUBLANES:(j + 1) * SUBLANES, :]
            for t0 in range(CONV_STRIDE):
                start = r0 + t0 + first_off + j
                if start not in wins:
                    wins[start] = ubuf[c][pl.ds(start, SUBLANES, stride=CONV_STRIDE), :]
                accs[t0] = accs[t0] + wj * wins[start]
        for t0 in range(CONV_STRIDE):
            cbuf[c][pl.ds(r0 + t0, SUBLANES, stride=CONV_STRIDE), :] = accs[t0]
        prev[chain] = accs
    for c in range(CONV_SLABS):
        ubuf[c][0:CONV_HALO, :] = ubuf[c][ts:ts + CONV_HALO, :]
    return jnp.concatenate([cbuf[c][...] for c in range(CONV_SLABS)], axis=-1)


def _level_operands(q, kk, f, b, bsc):
    ts = q.shape[0]
    row = lax.broadcasted_iota(jnp.int32, (ts, HG_WIDTH), 0)
    qf = q * f
    zs = []
    zs.append(jnp.where((row & 1) != 0, qf, kk))
    p = row & 3
    f_next = pltpu.roll(f, ts - 1, 0)
    f_prev = pltpu.roll(f, 1, 0)
    zs.append(jnp.where(p < 2, kk * jnp.where(p == 0, f_next, 1.0), qf * jnp.where(p == 3, f_prev, 1.0)))
    sub = lax.broadcasted_iota(jnp.int32, (SUBLANES, HG_WIDTH), 0)
    parts = []
    for g0 in range(0, ts, SUBLANES):
        bg = b[g0:g0 + SUBLANES]
        d = bg - bsc[g0 + 3:g0 + 4, :]
        e = jnp.exp(jnp.where(sub < 4, -d, d))
        parts.append(jnp.where(sub < 4, kk[g0:g0 + SUBLANES], q[g0:g0 + SUBLANES]) * e)
    zs.append(jnp.concatenate(parts, axis=0))
    m = 8
    while m < ts:
        parts = []
        for g0 in range(0, ts, 2 * m):
            br = bsc[g0 + m - 1:g0 + m, :]
            parts.append(kk[g0:g0 + m] * jnp.exp(br - b[g0:g0 + m]))
            parts.append(q[g0 + m:g0 + 2 * m] * jnp.exp(b[g0 + m:g0 + 2 * m] - br))
        zs.append(jnp.concatenate(parts, axis=0))
        m *= 2
    return [z.astype(_BF16) for z in zs]


def _mixer_kernel(x_ref, nw_ref, w_in_ref, dwwb_ref, dwbb_ref, lnw_ref, lnb_ref, wco_ref, bco_ref,
                  lb_ref, hnw_ref, who_ref, wout_ref, lvl_ref, tri_ref, out_ref,
                  *scratch):
    ts = TOKEN_TILE
    ubuf, cbuf = scratch[0:CONV_SLABS], scratch[CONV_SLABS:2 * CONV_SLABS]
    state, obuf, bsc = scratch[2 * CONV_SLABS:]

    @pl.when(pl.program_id(1) == 0)
    def _():
        for c in range(CONV_SLABS):
            ubuf[c][0:CONV_HALO, :] = jnp.zeros((CONV_HALO, LANES), _F32)
        state[...] = jnp.zeros_like(state)

    x = x_ref[0]
    hb = _rms(x, nw_ref[...]).astype(_BF16)

    ag = _dot(hb, w_in_ref[:, 0:2 * CONV_WIDTH])
    u = ag[:, :CONV_WIDTH] * jax.nn.sigmoid(ag[:, CONV_WIDTH:])
    never = lvl_ref[0:SUBLANES, :] == NEVER_LEVEL
    c = _causal_conv(u, dwwb_ref, dwbb_ref, ubuf, cbuf, never)
    hp = _dot(hb, w_in_ref[:, 2 * CONV_WIDTH:2 * CONV_WIDTH + 4 * HG_WIDTH])
    gates = _dot(hb, w_in_ref[:, 2 * CONV_WIDTH + 4 * HG_WIDTH:])
    tail = jnp.concatenate([hp[ts - SUBLANES:ts, :], gates[ts - SUBLANES:ts, :]], axis=1)
    tie = jnp.where(never, sum(tail[:, k * LANES:(k + 1) * LANES] for k in range(tail.shape[1] // LANES)), 0.0)
    lnb = lnb_ref[...]
    lnb = jnp.concatenate([lnb[:, :LANES] + tie[0:1, :], lnb[:, LANES:]], axis=1)
    mu = jnp.mean(c, axis=-1, keepdims=True)
    cc = c - mu
    var = jnp.mean(cc * cc, axis=-1, keepdims=True)
    cn = cc * lax.rsqrt(var + LN_EPS) * lnw_ref[...] + lnb
    cn = cn * jax.nn.sigmoid(cn)
    y_conv = _dot(cn.astype(_BF16), wco_ref[...]) + bco_ref[...]

    lbp = lb_ref[...]
    lmax = jnp.max(lbp, axis=0, keepdims=True)
    lexp = jnp.exp(lbp - lmax)
    lb = lexp[0:1, :] / jnp.sum(lexp, axis=0, keepdims=True)
    tri = tri_ref[...]
    lvl = lvl_ref[...]
    for r0 in range(0, ts, CHUNK):
        q = hp[r0:r0 + CHUNK, 0:HG_WIDTH]
        f_logit = hp[r0:r0 + CHUNK, HG_WIDTH:2 * HG_WIDTH]
        v = hp[r0:r0 + CHUNK, 2 * HG_WIDTH:3 * HG_WIDTH]
        g_out = hp[r0:r0 + CHUNK, 3 * HG_WIDTH:4 * HG_WIDTH]
        f = lb + (1.0 - lb) * jax.nn.sigmoid(f_logit)
        kk = 1.0 - f
        vb = v.astype(_BF16)

        g = jnp.log(f)
        g_hi = g.astype(_BF16)
        r1 = g - g_hi.astype(_F32)
        g_mid = r1.astype(_BF16)
        g_lo = (r1 - g_mid.astype(_F32)).astype(_BF16)
        b = _dot(tri, g_hi) + (_dot(tri, g_mid) + _dot(tri, g_lo))

        bsc[...] = b
        zs = _level_operands(q, kk, f, b, bsc)
        b_last = bsc[CHUNK - 1:CHUNK, :]
        q_in = (q * jnp.exp(b)).astype(_BF16)
        k_out = (kk * jnp.exp(b_last - b)).astype(_BF16)
        decay_all = jnp.exp(b_last)
        qk = q * kk
        n_diag_levels = len(zs) - 1
        for hh in range(HG_HEADS):
            cols = slice(hh * HG_HEAD_DIM, (hh + 1) * HG_HEAD_DIM)
            diag = []
            for rows in (slice(0, HALF), slice(HALF, CHUNK)):
                acc = jnp.zeros((HALF, HALF), _F32)
                for l in range(n_diag_levels):
                    z = zs[l][rows, cols]
                    acc = jnp.where(lvl == l, _dot_nt(z, z), acc)
                diag.append(acc.astype(_BF16))
            low = _dot_nt(zs[-1][HALF:CHUNK, cols], zs[-1][0:HALF, cols]).astype(_BF16)
            st = state[hh]
            o = _dot_nt(q_in[:, cols], st.astype(_BF16))
            o_a = _dot(diag[0], vb[0:HALF, cols])
            o_b = _dot(jnp.concatenate([low, diag[1]], axis=1), vb[:, cols])
            o = o + jnp.concatenate([o_a, o_b], axis=0)
            dg = jnp.sum(qk[:, cols], axis=-1, keepdims=True)
            o = o + dg * v[:, cols]
            vt = v[:, cols].T.astype(_BF16)
            state[hh] = st * decay_all[:, cols] + _dot(vt, k_out[:, cols])
            o = o * lax.rsqrt(jnp.mean(o * o, axis=-1, keepdims=True) + NORM_EPS)
            o = o * hnw_ref[:, cols]
            go = g_out[:, cols]
            obuf[r0:r0 + CHUNK, cols] = (o * (go * jax.nn.sigmoid(go))).astype(_BF16)
    y_rec = _dot(obuf[...], who_ref[...])

    y = jax.nn.sigmoid(gates[:, :D_MODEL]) * y_conv + jax.nn.sigmoid(gates[:, D_MODEL:]) * y_rec
    out_ref[0] = x + _dot(y.astype(_BF16), wout_ref[...])


def _mlp_kernel(x_ref, nw_ref, wup_ref, wdn_ref, fw_ref, out_ref, acc):
    x = x_ref[...]
    hb = _rms(x, nw_ref[...]).astype(_BF16)
    for ci in range(D_FF // FF_CHUNK):
        lo = ci * FF_CHUNK
        up = jnp.maximum(_dot(hb, wup_ref[:, lo:lo + FF_CHUNK]), 0.0)
        part = _dot((up * up).astype(_BF16), wdn_ref[lo:lo + FF_CHUNK, :])
        if ci == 0:
            acc[...] = x + part
        else:
            acc[...] += part
    out_ref[...] = _rms(acc[...], fw_ref[...])


def _const_spec(shape):
    nd = len(shape)
    return pl.BlockSpec(shape, lambda *_: (0,) * nd, pipeline_mode=pl.Buffered(1))


def _slabs(a):
    return a.reshape(a.shape[0], CONV_SLABS, LANES).transpose(1, 0, 2)


@jax.jit
def kernel(x, norm_mix_w, w_in, dw_conv_w, dw_conv_b, conv_ln_w, conv_ln_b, w_conv_out, b_conv_out, hgrn_lb, hgrn_norm_w, w_hgrn_out, w_out, norm_mlp_w, w_mlp_up, w_mlp_down, norm_final_w):
    batch, seq, d = x.shape
    assert d == D_MODEL and seq % TOKEN_TILE == 0 and TOKEN_TILE % CHUNK == 0 and (batch * seq) % MLP_TILE == 0
    ts = TOKEN_TILE
    row2 = lambda a: a.reshape(1, -1).astype(_F32)

    dww = dw_conv_w[0].reshape(CONV_KERNEL, CONV_WIDTH).astype(_F32)
    dwwb = _slabs(jnp.repeat(dww, SUBLANES, axis=0))
    dwbb = _slabs(jnp.broadcast_to(dw_conv_b[0].astype(_F32), (SUBLANES, CONV_WIDTH)))

    mixer_consts = [
        row2(norm_mix_w[0]),
        w_in[0].astype(_BF16),
        dwwb, dwbb, row2(conv_ln_w[0]), row2(conv_ln_b[0]),
        w_conv_out[0].astype(_BF16), row2(b_conv_out[0]),
        hgrn_lb.astype(_F32), row2(hgrn_norm_w[0]),
        w_hgrn_out[0].astype(_BF16), w_out[0].astype(_BF16),
        jnp.asarray(_pair_levels(HALF)),
        jnp.asarray(np.tril(np.ones((CHUNK, CHUNK), np.float32)), dtype=_BF16),
    ]
    x1 = pl.pallas_call(
        _mixer_kernel,
        grid=(batch, seq // ts),
        in_specs=[pl.BlockSpec((1, ts, d), lambda b, t: (b, t, 0))] + [_const_spec(a.shape) for a in mixer_consts],
        out_specs=pl.BlockSpec((1, ts, d), lambda b, t: (b, t, 0)),
        out_shape=jax.ShapeDtypeStruct(x.shape, _F32),
        scratch_shapes=[pltpu.VMEM((ts + CONV_HALO, LANES), _F32)] * CONV_SLABS
        + [pltpu.VMEM((ts, LANES), _F32)] * CONV_SLABS
        + [
            pltpu.VMEM((HG_HEADS, HG_HEAD_DIM, HG_HEAD_DIM), _F32),
            pltpu.VMEM((ts, HG_WIDTH), _BF16),
            pltpu.VMEM((CHUNK, HG_WIDTH), _F32),
        ],
        compiler_params=pltpu.CompilerParams(
            dimension_semantics=("arbitrary", "arbitrary"), vmem_limit_bytes=VMEM_LIMIT_BYTES),
        name="mixer",
    )(x, *mixer_consts)

    n_tok = batch * seq
    mlp_consts = [row2(norm_mlp_w[0]), w_mlp_up[0].astype(_BF16), w_mlp_down[0].astype(_BF16), row2(norm_final_w)]
    out = pl.pallas_call(
        _mlp_kernel,
        grid=(n_tok // MLP_TILE,),
        in_specs=[pl.BlockSpec((MLP_TILE, d), lambda i: (i, 0))] + [_const_spec(a.shape) for a in mlp_consts],
        out_specs=pl.BlockSpec((MLP_TILE, d), lambda i: (i, 0)),
        out_shape=jax.ShapeDtypeStruct((n_tok, d), _F32),
        scratch_shapes=[pltpu.VMEM((MLP_TILE, d), _F32)],
        compiler_params=pltpu.CompilerParams(
            dimension_semantics=("arbitrary",), vmem_limit_bytes=VMEM_LIMIT_BYTES),
        name="mlp",
    )(x1.reshape(n_tok, d), *mlp_consts)
    return out.reshape(batch, seq, d)
```

```python
import jax
import jax.numpy as jnp
import numpy as np
from jax import lax
from jax.experimental import pallas as pl
from jax.experimental.pallas import tpu as pltpu

D_MODEL = 1024
CONV_WIDTH = 512
CONV_KERNEL = 31
HG_WIDTH = 512
HG_HEAD_DIM = 128
HG_HEADS = HG_WIDTH // HG_HEAD_DIM
D_FF = 4096
NORM_EPS = 1e-6
LN_EPS = 1e-5

SUBLANES = 8
LANES = 128

TOKEN_TILE = 1024
CHUNK = 256
HALF = CHUNK // 2
CONV_HALO = 32
CONV_STRIDE = 4
CONV_ROW_BLOCK = SUBLANES * CONV_STRIDE
CONV_SLABS = CONV_WIDTH // LANES
CONV_CHAINS = 1
NEVER_LEVEL = 1 << 20
MLP_TILE = 1024
FF_CHUNK = 1024
VMEM_LIMIT_BYTES = 63 * 1024 * 1024

_BF16 = jnp.bfloat16
_F32 = jnp.float32


def _dot(a, b):
    return jnp.dot(a, b, preferred_element_type=_F32)


def _dot_nt(a, b):
    return lax.dot_general(a, b, (((1,), (1,)), ((), ())), preferred_element_type=_F32)


def _pair_levels(n):
    t = np.arange(n)[:, None]
    s = np.arange(n)[None, :]
    lvl = np.floor(np.log2(np.maximum(t ^ s, 1))).astype(np.int32)
    return np.where(t > s, lvl, -1).astype(np.int32)


def _rms(x, w):
    return x * lax.rsqrt(jnp.mean(x * x, axis=-1, keepdims=True) + NORM_EPS) * w


def _causal_conv(u, dwwb_ref, dwbb_ref, ubuf, cbuf, never):
    ts = TOKEN_TILE
    for c in range(CONV_SLABS):
        ubuf[c][CONV_HALO:CONV_HALO + ts, :] = u[:, c * LANES:(c + 1) * LANES]
    first_off = CONV_HALO - (CONV_KERNEL - 1)
    units = [(r0, c) for r0 in range(0, ts, CONV_ROW_BLOCK) for c in range(CONV_SLABS)]
    prev = [None] * CONV_CHAINS
    for n, (r0, c) in enumerate(units):
        chain = n % CONV_CHAINS
        bias = dwbb_ref[c]
        if prev[chain] is None:
            accs = [bias] * CONV_STRIDE
        else:
            accs = [jnp.where(never, p, bias) for p in prev[chain]]
        wins = {}
        for j in range(CONV_KERNEL):
            wj = dwwb_ref[c, j * SUBLANES:(j + 1) * SUBLANES, :]
            for t0 in range(CONV_STRIDE):
                start = r0 + t0 + first_off + j
                if start not in wins:
                    wins[start] = ubuf[c][pl.ds(start, SUBLANES, stride=CONV_STRIDE), :]
                accs[t0] = accs[t0] + wj * wins[start]
        for t0 in range(CONV_STRIDE):
            cbuf[c][pl.ds(r0 + t0, SUBLANES, stride=CONV_STRIDE), :] = accs[t0]
        prev[chain] = accs
    for c in range(CONV_SLABS):
        ubuf[c][0:CONV_HALO, :] = ubuf[c][ts:ts + CONV_HALO, :]
    return jnp.concatenate([cbuf[c][...] for c in range(CONV_SLABS)], axis=-1)


def _level_operands(q, kk, f, b, bsc):
    ts = q.shape[0]
    row = lax.broadcasted_iota(jnp.int32, (ts, HG_WIDTH), 0)
    qf = q * f
    zs = []
    zs.append(jnp.where((row & 1) != 0, qf, kk))
    p = row & 3
    f_next = pltpu.roll(f, ts - 1, 0)
    f_prev = pltpu.roll(f, 1, 0)
    zs.append(jnp.where(p < 2, kk * jnp.where(p == 0, f_next, 1.0), qf * jnp.where(p == 3, f_prev, 1.0)))
    sub = lax.broadcasted_iota(jnp.int32, (SUBLANES, HG_WIDTH), 0)
    parts = []
    for g0 in range(0, ts, SUBLANES):
        bg = b[g0:g0 + SUBLANES]
        d = bg - bsc[g0 + 3:g0 + 4, :]
        e = jnp.exp(jnp.where(sub < 4, -d, d))
        parts.append(jnp.where(sub < 4, kk[g0:g0 + SUBLANES], q[g0:g0 + SUBLANES]) * e)
    zs.append(jnp.concatenate(parts, axis=0))
    m = 8
    while m < ts:
        parts = []
        for g0 in range(0, ts, 2 * m):
            br = bsc[g0 + m - 1:g0 + m, :]
            parts.append(kk[g0:g0 + m] * jnp.exp(br - b[g0:g0 + m]))
            parts.append(q[g0 + m:g0 + 2 * m] * jnp.exp(b[g0 + m:g0 + 2 * m] - br))
        zs.append(jnp.concatenate(parts, axis=0))
        m *= 2
    return [z.astype(_BF16) for z in zs]


def _mixer_kernel(x_ref, nw_ref, w_in_ref, dwwb_ref, dwbb_ref, lnw_ref, lnb_ref, wco_ref, bco_ref,
                  lb_ref, hnw_ref, who_ref, wout_ref, lvl_ref, tri_ref, out_ref,
                  *scratch):
    ts = TOKEN_TILE
    ubuf, cbuf = scratch[0:CONV_SLABS], scratch[CONV_SLABS:2 * CONV_SLABS]
    state, obuf, bsc = scratch[2 * CONV_SLABS:]

    @pl.when(pl.program_id(1) == 0)
    def _():
        for c in range(CONV_SLABS):
            ubuf[c][0:CONV_HALO, :] = jnp.zeros((CONV_HALO, LANES), _F32)
        state[...] = jnp.zeros_like(state)

    x = x_ref[0]
    hb = _rms(x, nw_ref[...]).astype(_BF16)

    ag = _dot(hb, w_in_ref[:, 0:2 * CONV_WIDTH])
    a_half = ag[:, :CONV_WIDTH]
    u = a_half + a_half * jnp.tanh(ag[:, CONV_WIDTH:])
    never = lvl_ref[0:SUBLANES, :] == NEVER_LEVEL
    c = _causal_conv(u, dwwb_ref, dwbb_ref, ubuf, cbuf, never)
    hp = _dot(hb, w_in_ref[:, 2 * CONV_WIDTH:2 * CONV_WIDTH + 4 * HG_WIDTH])
    gates = _dot(hb, w_in_ref[:, 2 * CONV_WIDTH + 4 * HG_WIDTH:])
    tail = jnp.concatenate([hp[ts - SUBLANES:ts, :], gates[ts - SUBLANES:ts, :]], axis=1)
    tie = jnp.where(never, sum(tail[:, k * LANES:(k + 1) * LANES] for k in range(tail.shape[1] // LANES)), 0.0)
    lnb = 0.5 * lnb_ref[...]
    lnb = jnp.concatenate([lnb[:, :LANES] + tie[0:1, :], lnb[:, LANES:]], axis=1)
    mu = jnp.mean(c, axis=-1, keepdims=True)
    cc = c - mu
    var = jnp.mean(cc * cc, axis=-1, keepdims=True)
    cn_half = cc * lax.rsqrt(var + LN_EPS) * (0.5 * lnw_ref[...]) + lnb
    cn = cn_half + cn_half * jnp.tanh(cn_half)
    y_conv = _dot(cn.astype(_BF16), wco_ref[...]) + bco_ref[...]

    lbp = lb_ref[...]
    lmax = jnp.max(lbp, axis=0, keepdims=True)
    lexp = jnp.exp(lbp - lmax)
    lb = lexp[0:1, :] / jnp.sum(lexp, axis=0, keepdims=True)
    f_mid, f_amp = 0.5 * (1.0 + lb), 0.5 * (1.0 - lb)
    tri = tri_ref[...]
    lvl = lvl_ref[...]
    for r0 in range(0, ts, CHUNK):
        q = hp[r0:r0 + CHUNK, 0:HG_WIDTH]
        f_logit_half = hp[r0:r0 + CHUNK, HG_WIDTH:2 * HG_WIDTH]
        v = hp[r0:r0 + CHUNK, 2 * HG_WIDTH:3 * HG_WIDTH]
        g_out_half = hp[r0:r0 + CHUNK, 3 * HG_WIDTH:4 * HG_WIDTH]
        f = f_mid + f_amp * jnp.tanh(f_logit_half)
        kk = 1.0 - f
        vb = v.astype(_BF16)

        g = jnp.log(f)
        g_hi = g.astype(_BF16)
        r1 = g - g_hi.astype(_F32)
        g_mid = r1.astype(_BF16)
        g_lo = (r1 - g_mid.astype(_F32)).astype(_BF16)
        b = _dot(tri, g_hi) + (_dot(tri, g_mid) + _dot(tri, g_lo))

        bsc[...] = b
        zs = _level_operands(q, kk, f, b, bsc)
        b_last = bsc[CHUNK - 1:CHUNK, :]
        q_in = (q * jnp.exp(b)).astype(_BF16)
        k_out = (kk * jnp.exp(b_last - b)).astype(_BF16)
        decay_all = jnp.exp(b_last)
        qk = q * kk
        n_diag_levels = len(zs) - 1
        for hh in range(HG_HEADS):
            cols = slice(hh * HG_HEAD_DIM, (hh + 1) * HG_HEAD_DIM)
            diag = []
            for rows in (slice(0, HALF), slice(HALF, CHUNK)):
                acc = jnp.zeros((HALF, HALF), _F32)
                for l in range(n_diag_levels):
                    z = zs[l][rows, cols]
                    acc = jnp.where(lvl == l, _dot_nt(z, z), acc)
                diag.append(acc.astype(_BF16))
            low = _dot_nt(zs[-1][HALF:CHUNK, cols], zs[-1][0:HALF, cols]).astype(_BF16)
            st = state[hh]
            o = _dot_nt(q_in[:, cols], st.astype(_BF16))
            o_a = _dot(diag[0], vb[0:HALF, cols])
            o_b = _dot(jnp.concatenate([low, diag[1]], axis=1), vb[:, cols])
            o = o + jnp.concatenate([o_a, o_b], axis=0)
            dg = jnp.sum(qk[:, cols], axis=-1, keepdims=True)
            o = o + dg * v[:, cols]
            vt = v[:, cols].T.astype(_BF16)
            state[hh] = st * decay_all[:, cols] + _dot(vt, k_out[:, cols])
            o = o * lax.rsqrt(jnp.mean(o * o, axis=-1, keepdims=True) + NORM_EPS)
            o = o * hnw_ref[:, cols]
            gh = g_out_half[:, cols]
            obuf[r0:r0 + CHUNK, cols] = (o * (gh + gh * jnp.tanh(gh))).astype(_BF16)
    y_rec = _dot(obuf[...], who_ref[...])

    y2 = (y_conv + y_rec) + (jnp.tanh(gates[:, :D_MODEL]) * y_conv + jnp.tanh(gates[:, D_MODEL:]) * y_rec)
    out_ref[0] = x + _dot(y2.astype(_BF16), wout_ref[...])


def _mlp_kernel(x_ref, nw_ref, wup_ref, wdn_ref, fw_ref, out_ref, acc):
    x = x_ref[...]
    hb = _rms(x, nw_ref[...]).astype(_BF16)
    for ci in range(D_FF // FF_CHUNK):
        lo = ci * FF_CHUNK
        up = jnp.maximum(_dot(hb, wup_ref[:, lo:lo + FF_CHUNK]), 0.0)
        part = _dot((up * up).astype(_BF16), wdn_ref[lo:lo + FF_CHUNK, :])
        if ci == 0:
            acc[...] = x + part
        else:
            acc[...] += part
    out_ref[...] = _rms(acc[...], fw_ref[...])


def _const_spec(shape):
    nd = len(shape)
    return pl.BlockSpec(shape, lambda *_: (0,) * nd, pipeline_mode=pl.Buffered(1))


def _slabs(a):
    return a.reshape(a.shape[0], CONV_SLABS, LANES).transpose(1, 0, 2)


@jax.jit
def kernel(x, norm_mix_w, w_in, dw_conv_w, dw_conv_b, conv_ln_w, conv_ln_b, w_conv_out, b_conv_out, hgrn_lb, hgrn_norm_w, w_hgrn_out, w_out, norm_mlp_w, w_mlp_up, w_mlp_down, norm_final_w):
    batch, seq, d = x.shape
    assert d == D_MODEL and seq % TOKEN_TILE == 0 and TOKEN_TILE % CHUNK == 0 and (batch * seq) % MLP_TILE == 0
    ts = TOKEN_TILE
    row2 = lambda a: a.reshape(1, -1).astype(_F32)

    dww = dw_conv_w[0].reshape(CONV_KERNEL, CONV_WIDTH).astype(_F32)
    dwwb = _slabs(jnp.repeat(dww, SUBLANES, axis=0))
    dwbb = _slabs(jnp.broadcast_to(dw_conv_b[0].astype(_F32), (SUBLANES, CONV_WIDTH)))

    col = np.arange(w_in.shape[-1])
    q_cols = (col >= 2 * CONV_WIDTH) & (col < 2 * CONV_WIDTH + HG_WIDTH)
    v_cols = (col >= 2 * CONV_WIDTH + 2 * HG_WIDTH) & (col < 2 * CONV_WIDTH + 3 * HG_WIDTH)
    in_scale = jnp.asarray(np.where(q_cols | v_cols, 1.0, 0.5), _F32)
    mixer_consts = [
        row2(norm_mix_w[0]),
        (w_in[0] * in_scale).astype(_BF16),
        dwwb, dwbb, row2(conv_ln_w[0]), row2(conv_ln_b[0]),
        w_conv_out[0].astype(_BF16), row2(b_conv_out[0]),
        hgrn_lb.astype(_F32), row2(hgrn_norm_w[0]),
        w_hgrn_out[0].astype(_BF16), (0.5 * w_out[0]).astype(_BF16),
        jnp.asarray(_pair_levels(HALF)),
        jnp.asarray(np.tril(np.ones((CHUNK, CHUNK), np.float32)), dtype=_BF16),
    ]
    x1 = pl.pallas_call(
        _mixer_kernel,
        grid=(batch, seq // ts),
        in_specs=[pl.BlockSpec((1, ts, d), lambda b, t: (b, t, 0))] + [_const_spec(a.shape) for a in mixer_consts],
        out_specs=pl.BlockSpec((1, ts, d), lambda b, t: (b, t, 0)),
        out_shape=jax.ShapeDtypeStruct(x.shape, _F32),
        scratch_shapes=[pltpu.VMEM((ts + CONV_HALO, LANES), _F32)] * CONV_SLABS
        + [pltpu.VMEM((ts, LANES), _F32)] * CONV_SLABS
        + [
            pltpu.VMEM((HG_HEADS, HG_HEAD_DIM, HG_HEAD_DIM), _F32),
            pltpu.VMEM((ts, HG_WIDTH), _BF16),
            pltpu.VMEM((CHUNK, HG_WIDTH), _F32),
        ],
        compiler_params=pltpu.CompilerParams(
            dimension_semantics=("arbitrary", "arbitrary"), vmem_limit_bytes=VMEM_LIMIT_BYTES),
        name="mixer",
    )(x, *mixer_consts)

    n_tok = batch * seq
    mlp_consts = [row2(norm_mlp_w[0]), w_mlp_up[0].astype(_BF16), w_mlp_down[0].astype(_BF16), row2(norm_final_w)]
    out = pl.pallas_call(
        _mlp_kernel,
        grid=(n_tok // MLP_TILE,),
        in_specs=[pl.BlockSpec((MLP_TILE, d), lambda i: (i, 0))] + [_const_spec(a.shape) for a in mlp_consts],
        out_specs=pl.BlockSpec((MLP_TILE, d), lambda i: (i, 0)),
        out_shape=jax.ShapeDtypeStruct((n_tok, d), _F32),
        scratch_shapes=[pltpu.VMEM((MLP_TILE, d), _F32)],
        compiler_params=pltpu.CompilerParams(
            dimension_semantics=("arbitrary",), vmem_limit_bytes=VMEM_LIMIT_BYTES),
        name="mlp",
    )(x1.reshape(n_tok, d), *mlp_consts)
    return out.reshape(batch, seq, d)
```

```python
import jax
import jax.numpy as jnp
import numpy as np
from jax import lax
from jax.experimental import pallas as pl
from jax.experimental.pallas import tpu as pltpu

D_MODEL = 1024
CONV_WIDTH = 512
CONV_KERNEL = 31
HG_WIDTH = 512
HG_HEAD_DIM = 128
HG_HEADS = HG_WIDTH // HG_HEAD_DIM
D_FF = 4096
NORM_EPS = 1e-6
LN_EPS = 1e-5

SUBLANES = 8
LANES = 128

TOKEN_TILE = 1024
CHUNK = 256
HALF = CHUNK // 2
CONV_HALO = 32
CONV_STRIDE = 4
CONV_ROW_BLOCK = SUBLANES * CONV_STRIDE
CONV_SLABS = CONV_WIDTH // LANES
CONV_CHAINS = 1
NEVER_LEVEL = 1 << 20
MLP_TILE = 1024
FF_CHUNK = 1024
VMEM_LIMIT_BYTES = 63 * 1024 * 1024

_BF16 = jnp.bfloat16
_F32 = jnp.float32


def _dot(a, b):
    return jnp.dot(a, b, preferred_element_type=_F32)


def _dot_nt(a, b):
    return lax.dot_general(a, b, (((1,), (1,)), ((), ())), preferred_element_type=_F32)


def _pair_levels(n):
    t = np.arange(n)[:, None]
    s = np.arange(n)[None, :]
    lvl = np.floor(np.log2(np.maximum(t ^ s, 1))).astype(np.int32)
    return np.where(t > s, lvl, -1).astype(np.int32)


def _rms(x, w):
    return x * lax.rsqrt(jnp.mean(x * x, axis=-1, keepdims=True) + NORM_EPS) * w


def _causal_conv(u, dwwb_ref, dwbb_ref, ubuf, cbuf, never):
    ts = TOKEN_TILE
    for c in range(CONV_SLABS):
        ubuf[c][CONV_HALO:CONV_HALO + ts, :] = u[:, c * LANES:(c + 1) * LANES]
    first_off = CONV_HALO - (CONV_KERNEL - 1)
    units = [(r0, c) for r0 in range(0, ts, CONV_ROW_BLOCK) for c in range(CONV_SLABS)]
    prev = [None] * CONV_CHAINS
    for n, (r0, c) in enumerate(units):
        chain = n % CONV_CHAINS
        bias = dwbb_ref[c]
        if prev[chain] is None:
            accs = [bias] * CONV_STRIDE
        else:
            accs = [jnp.where(never, p, bias) for p in prev[chain]]
        wins = {}
        for j in range(CONV_KERNEL):
            wj = dwwb_ref[c, j * SUBLANES:(j + 1) * SUBLANES, :]
            for t0 in range(CONV_STRIDE):
                start = r0 + t0 + first_off + j
                if start not in wins:
                    wins[start] = ubuf[c][pl.ds(start, SUBLANES, stride=CONV_STRIDE), :]
                accs[t0] = accs[t0] + wj * wins[start]
        for t0 in range(CONV_STRIDE):
            cbuf[c][pl.ds(r0 + t0, SUBLANES, stride=CONV_STRIDE), :] = accs[t0]
        prev[chain] = accs
    for c in range(CONV_SLABS):
        ubuf[c][0:CONV_HALO, :] = ubuf[c][ts:ts + CONV_HALO, :]
    return jnp.concatenate([cbuf[c][...] for c in range(CONV_SLABS)], axis=-1)


def _level_operands(q, kk, f, b, bsc):
    ts = q.shape[0]
    row = lax.broadcasted_iota(jnp.int32, (ts, HG_WIDTH), 0)
    qf = q * f
    zs = []
    zs.append(jnp.where((row & 1) != 0, qf, kk))
    p = row & 3
    f_next = pltpu.roll(f, ts - 1, 0)
    f_prev = pltpu.roll(f, 1, 0)
    zs.append(jnp.where(p < 2, kk * jnp.where(p == 0, f_next, 1.0), qf * jnp.where(p == 3, f_prev, 1.0)))
    sub = lax.broadcasted_iota(jnp.int32, (SUBLANES, HG_WIDTH), 0)
    parts = []
    for g0 in range(0, ts, SUBLANES):
        bg = b[g0:g0 + SUBLANES]
        d = bg - bsc[g0 + 3:g0 + 4, :]
        e = jnp.exp(jnp.where(sub < 4, -d, d))
        parts.append(jnp.where(sub < 4, kk[g0:g0 + SUBLANES], q[g0:g0 + SUBLANES]) * e)
    zs.append(jnp.concatenate(parts, axis=0))
    m = 8
    while m < ts:
        parts = []
        for g0 in range(0, ts, 2 * m):
            br = bsc[g0 + m - 1:g0 + m, :]
            parts.append(kk[g0:g0 + m] * jnp.exp(br - b[g0:g0 + m]))
            parts.append(q[g0 + m:g0 + 2 * m] * jnp.exp(b[g0 + m:g0 + 2 * m] - br))
        zs.append(jnp.concatenate(parts, axis=0))
        m *= 2
    return [z.astype(_BF16) for z in zs]


def _mixer_kernel(x_ref, nw_ref, w_in_ref, dwwb_ref, dwbb_ref, lnw_ref, lnb_ref, wco_ref, bco_ref,
                  lb_ref, hnw_ref, who_ref, wout_ref, lvl_ref, tri_ref, out_ref,
                  *scratch):
    ts = TOKEN_TILE
    ubuf, cbuf = scratch[0:CONV_SLABS], scratch[CONV_SLABS:2 * CONV_SLABS]
    state, obuf, bsc = scratch[2 * CONV_SLABS:]

    @pl.when(pl.program_id(1) == 0)
    def _():
        for c in range(CONV_SLABS):
            ubuf[c][0:CONV_HALO, :] = jnp.zeros((CONV_HALO, LANES), _F32)
        state[...] = jnp.zeros_like(state)

    x = x_ref[0]
    hb = _rms(x, nw_ref[...]).astype(_BF16)

    ag = _dot(hb, w_in_ref[:, 0:2 * CONV_WIDTH])
    a_half = ag[:, :CONV_WIDTH]
    u = a_half + a_half * jnp.tanh(ag[:, CONV_WIDTH:])
    never = lvl_ref[0:SUBLANES, :] == NEVER_LEVEL
    c = _causal_conv(u, dwwb_ref, dwbb_ref, ubuf, cbuf, never)
    hp = _dot(hb, w_in_ref[:, 2 * CONV_WIDTH:2 * CONV_WIDTH + 4 * HG_WIDTH])
    gates = _dot(hb, w_in_ref[:, 2 * CONV_WIDTH + 4 * HG_WIDTH:])
    mu = jnp.mean(c, axis=-1, keepdims=True)
    cc = c - mu
    var = jnp.mean(cc * cc, axis=-1, keepdims=True)
    cn_half = cc * lax.rsqrt(var + LN_EPS) * (0.5 * lnw_ref[...]) + 0.5 * lnb_ref[...]
    cn = cn_half + cn_half * jnp.tanh(cn_half)
    y_conv = _dot(cn.astype(_BF16), wco_ref[...]) + bco_ref[...]

    lbp = lb_ref[...]
    lmax = jnp.max(lbp, axis=0, keepdims=True)
    lexp = jnp.exp(lbp - lmax)
    lb = lexp[0:1, :] / jnp.sum(lexp, axis=0, keepdims=True)
    f_mid, f_amp = 0.5 * (1.0 + lb), 0.5 * (1.0 - lb)
    tri = tri_ref[...]
    lvl = lvl_ref[...]
    for r0 in range(0, ts, CHUNK):
        q = hp[r0:r0 + CHUNK, 0:HG_WIDTH]
        f_logit_half = hp[r0:r0 + CHUNK, HG_WIDTH:2 * HG_WIDTH]
        v = hp[r0:r0 + CHUNK, 2 * HG_WIDTH:3 * HG_WIDTH]
        g_out_half = hp[r0:r0 + CHUNK, 3 * HG_WIDTH:4 * HG_WIDTH]
        f = f_mid + f_amp * jnp.tanh(f_logit_half)
        kk = 1.0 - f
        vb = v.astype(_BF16)

        g = jnp.log(f)
        g_hi = g.astype(_BF16)
        r1 = g - g_hi.astype(_F32)
        g_mid = r1.astype(_BF16)
        g_lo = (r1 - g_mid.astype(_F32)).astype(_BF16)
        b = _dot(tri, g_hi) + (_dot(tri, g_mid) + _dot(tri, g_lo))

        bsc[...] = b
        zs = _level_operands(q, kk, f, b, bsc)
        b_last = bsc[CHUNK - 1:CHUNK, :]
        q_in = (q * jnp.exp(b)).astype(_BF16)
        k_out = (kk * jnp.exp(b_last - b)).astype(_BF16)
        decay_all = jnp.exp(b_last)
        qk = q * kk
        n_diag_levels = len(zs) - 1
        for hh in range(HG_HEADS):
            cols = slice(hh * HG_HEAD_DIM, (hh + 1) * HG_HEAD_DIM)
            diag = []
            for rows in (slice(0, HALF), slice(HALF, CHUNK)):
                acc = jnp.zeros((HALF, HALF), _F32)
                for l in range(n_diag_levels):
                    z = zs[l][rows, cols]
                    acc = jnp.where(lvl == l, _dot_nt(z, z), acc)
                diag.append(acc.astype(_BF16))
            low = _dot_nt(zs[-1][HALF:CHUNK, cols], zs[-1][0:HALF, cols]).astype(_BF16)
            st = state[hh]
            o = _dot_nt(q_in[:, cols], st.astype(_BF16))
            o_a = _dot(diag[0], vb[0:HALF, cols])
            o_b = _dot(jnp.concatenate([low, diag[1]], axis=1), vb[:, cols])
            o = o + jnp.concatenate([o_a, o_b], axis=0)
            dg = jnp.sum(qk[:, cols], axis=-1, keepdims=True)
            o = o + dg * v[:, cols]
            vt = v[:, cols].T.astype(_BF16)
            state[hh] = st * decay_all[:, cols] + _dot(vt, k_out[:, cols])
            o = o * lax.rsqrt(jnp.mean(o * o, axis=-1, keepdims=True) + NORM_EPS)
            o = o * hnw_ref[:, cols]
            gh = g_out_half[:, cols]
            obuf[r0:r0 + CHUNK, cols] = (o * (gh + gh * jnp.tanh(gh))).astype(_BF16)
    y_rec = _dot(obuf[...], who_ref[...])

    y2 = (y_conv + y_rec) + (jnp.tanh(gates[:, :D_MODEL]) * y_conv + jnp.tanh(gates[:, D_MODEL:]) * y_rec)
    out_ref[0] = x + _dot(y2.astype(_BF16), wout_ref[...])


def _mlp_kernel(x_ref, nw_ref, wup_ref, wdn_ref, fw_ref, out_ref, acc):
    x = x_ref[...]
    hb = _rms(x, nw_ref[...]).astype(_BF16)
    for ci in range(D_FF // FF_CHUNK):
        lo = ci * FF_CHUNK
        up = jnp.maximum(_dot(hb, wup_ref[:, lo:lo + FF_CHUNK]), 0.0)
        part = _dot((up * up).astype(_BF16), wdn_ref[lo:lo + FF_CHUNK, :])
        if ci == 0:
            acc[...] = x + part
        else:
            acc[...] += part
    out_ref[...] = _rms(acc[...], fw_ref[...])


def _const_spec(shape):
    nd = len(shape)
    return pl.BlockSpec(shape, lambda *_: (0,) * nd, pipeline_mode=pl.Buffered(1))


def _slabs(a):
    return a.reshape(a.shape[0], CONV_SLABS, LANES).transpose(1, 0, 2)


@jax.jit
def kernel(x, norm_mix_w, w_in, dw_conv_w, dw_conv_b, conv_ln_w, conv_ln_b, w_conv_out, b_conv_out, hgrn_lb, hgrn_norm_w, w_hgrn_out, w_out, norm_mlp_w, w_mlp_up, w_mlp_down, norm_final_w):
    batch, seq, d = x.shape
    assert d == D_MODEL and seq % TOKEN_TILE == 0 and TOKEN_TILE % CHUNK == 0 and (batch * seq) % MLP_TILE == 0
    ts = TOKEN_TILE
    row2 = lambda a: a.reshape(1, -1).astype(_F32)

    dww = dw_conv_w[0].reshape(CONV_KERNEL, CONV_WIDTH).astype(_F32)
    dwwb = _slabs(jnp.repeat(dww, SUBLANES, axis=0))
    dwbb = _slabs(jnp.broadcast_to(dw_conv_b[0].astype(_F32), (SUBLANES, CONV_WIDTH)))

    col = np.arange(w_in.shape[-1])
    q_cols = (col >= 2 * CONV_WIDTH) & (col < 2 * CONV_WIDTH + HG_WIDTH)
    v_cols = (col >= 2 * CONV_WIDTH + 2 * HG_WIDTH) & (col < 2 * CONV_WIDTH + 3 * HG_WIDTH)
    in_scale = jnp.asarray(np.where(q_cols | v_cols, 1.0, 0.5), _F32)
    mixer_consts = [
        row2(norm_mix_w[0]),
        (w_in[0] * in_scale).astype(_BF16),
        dwwb, dwbb, row2(conv_ln_w[0]), row2(conv_ln_b[0]),
        w_conv_out[0].astype(_BF16), row2(b_conv_out[0]),
        hgrn_lb.astype(_F32), row2(hgrn_norm_w[0]),
        w_hgrn_out[0].astype(_BF16), (0.5 * w_out[0]).astype(_BF16),
        jnp.asarray(_pair_levels(HALF)),
        jnp.asarray(np.tril(np.ones((CHUNK, CHUNK), np.float32)), dtype=_BF16),
    ]
    x1 = pl.pallas_call(
        _mixer_kernel,
        grid=(batch, seq // ts),
        in_specs=[pl.BlockSpec((1, ts, d), lambda b, t: (b, t, 0))] + [_const_spec(a.shape) for a in mixer_consts],
        out_specs=pl.BlockSpec((1, ts, d), lambda b, t: (b, t, 0)),
        out_shape=jax.ShapeDtypeStruct(x.shape, _F32),
        scratch_shapes=[pltpu.VMEM((ts + CONV_HALO, LANES), _F32)] * CONV_SLABS
        + [pltpu.VMEM((ts, LANES), _F32)] * CONV_SLABS
        + [
            pltpu.VMEM((HG_HEADS, HG_HEAD_DIM, HG_HEAD_DIM), _F32),
            pltpu.VMEM((ts, HG_WIDTH), _BF16),
            pltpu.VMEM((CHUNK, HG_WIDTH), _F32),
        ],
        compiler_params=pltpu.CompilerParams(
            dimension_semantics=("arbitrary", "arbitrary"), vmem_limit_bytes=VMEM_LIMIT_BYTES),
        name="mixer",
    )(x, *mixer_consts)

    n_tok = batch * seq
    mlp_consts = [row2(norm_mlp_w[0]), w_mlp_up[0].astype(_BF16), w_mlp_down[0].astype(_BF16), row2(norm_final_w)]
    out = pl.pallas_call(
        _mlp_kernel,
        grid=(n_tok // MLP_TILE,),
        in_specs=[pl.BlockSpec((MLP_TILE, d), lambda i: (i, 0))] + [_const_spec(a.shape) for a in mlp_consts],
        out_specs=pl.BlockSpec((MLP_TILE, d), lambda i: (i, 0)),
        out_shape=jax.ShapeDtypeStruct((n_tok, d), _F32),
        scratch_shapes=[pltpu.VMEM((MLP_TILE, d), _F32)],
        compiler_params=pltpu.CompilerParams(
            dimension_semantics=("arbitrary",), vmem_limit_bytes=VMEM_LIMIT_BYTES),
        name="mlp",
    )(x1.reshape(n_tok, d), *mlp_consts)
    return out.reshape(batch, seq, d)
```

```python
import jax
import jax.numpy as jnp
import numpy as np
from jax import lax
from jax.experimental import pallas as pl
from jax.experimental.pallas import tpu as pltpu

D_MODEL = 1024
CONV_WIDTH = 512
CONV_KERNEL = 31
HG_WIDTH = 512
HG_HEAD_DIM = 128
HG_HEADS = HG_WIDTH // HG_HEAD_DIM
D_FF = 4096
NORM_EPS = 1e-6
LN_EPS = 1e-5

SUBLANES = 8
LANES = 128

TOKEN_TILE = 1024
CHUNK = 256
HALF = CHUNK // 2
CONV_HALO = 32
CONV_STRIDE = 4
CONV_ROW_BLOCK = SUBLANES * CONV_STRIDE
CONV_SLABS = CONV_WIDTH // LANES
CONV_CHAINS = 1
NEVER_LEVEL = 1 << 20
MLP_TILE = 1024
FF_CHUNK = 1024
VMEM_LIMIT_BYTES = 63 * 1024 * 1024

_BF16 = jnp.bfloat16
_F32 = jnp.float32


def _dot(a, b):
    return jnp.dot(a, b, preferred_element_type=_F32)


def _dot_nt(a, b):
    return lax.dot_general(a, b, (((1,), (1,)), ((), ())), preferred_element_type=_F32)


def _pair_levels(n):
    t = np.arange(n)[:, None]
    s = np.arange(n)[None, :]
    lvl = np.floor(np.log2(np.maximum(t ^ s, 1))).astype(np.int32)
    return np.where(t > s, lvl, -1).astype(np.int32)


def _rms(x, w):
    return x * lax.rsqrt(jnp.mean(x * x, axis=-1, keepdims=True) + NORM_EPS) * w


def _causal_conv(u, dwwb_ref, dwbb_ref, ubuf, cbuf, never):
    ts = TOKEN_TILE
    for c in range(CONV_SLABS):
        ubuf[c][CONV_HALO:CONV_HALO + ts, :] = u[:, c * LANES:(c + 1) * LANES]
    first_off = CONV_HALO - (CONV_KERNEL - 1)
    units = [(r0, c) for r0 in range(0, ts, CONV_ROW_BLOCK) for c in range(CONV_SLABS)]
    prev = [None] * CONV_CHAINS
    for n, (r0, c) in enumerate(units):
        chain = n % CONV_CHAINS
        bias = dwbb_ref[c]
        if prev[chain] is None:
            accs = [bias] * CONV_STRIDE
        else:
            accs = [jnp.where(never, p, bias) for p in prev[chain]]
        wins = {}
        for j in range(CONV_KERNEL):
            wj = dwwb_ref[c, j * SUBLANES:(j + 1) * SUBLANES, :]
            for t0 in range(CONV_STRIDE):
                start = r0 + t0 + first_off + j
                if start not in wins:
                    wins[start] = ubuf[c][pl.ds(start, SUBLANES, stride=CONV_STRIDE), :]
                accs[t0] = accs[t0] + wj * wins[start]
        for t0 in range(CONV_STRIDE):
            cbuf[c][pl.ds(r0 + t0, SUBLANES, stride=CONV_STRIDE), :] = accs[t0]
        prev[chain] = accs
    for c in range(CONV_SLABS):
        ubuf[c][0:CONV_HALO, :] = ubuf[c][ts:ts + CONV_HALO, :]
    return jnp.concatenate([cbuf[c][...] for c in range(CONV_SLABS)], axis=-1)


def _level_operands(q, kk, f, b, bsc):
    ts = q.shape[0]
    row = lax.broadcasted_iota(jnp.int32, (ts, HG_WIDTH), 0)
    qf = q * f
    zs = []
    zs.append(jnp.where((row & 1) != 0, qf, kk))
    p = row & 3
    f_next = pltpu.roll(f, ts - 1, 0)
    f_prev = pltpu.roll(f, 1, 0)
    zs.append(jnp.where(p < 2, kk * jnp.where(p == 0, f_next, 1.0), qf * jnp.where(p == 3, f_prev, 1.0)))
    sub = lax.broadcasted_iota(jnp.int32, (SUBLANES, HG_WIDTH), 0)
    parts = []
    for g0 in range(0, ts, SUBLANES):
        bg = b[g0:g0 + SUBLANES]
        d = bg - bsc[g0 + 3:g0 + 4, :]
        e = jnp.exp(jnp.where(sub < 4, -d, d))
        parts.append(jnp.where(sub < 4, kk[g0:g0 + SUBLANES], q[g0:g0 + SUBLANES]) * e)
    zs.append(jnp.concatenate(parts, axis=0))
    m = 8
    while m < ts:
        parts = []
        for g0 in range(0, ts, 2 * m):
            br = bsc[g0 + m - 1:g0 + m, :]
            parts.append(kk[g0:g0 + m] * jnp.exp(br - b[g0:g0 + m]))
            parts.append(q[g0 + m:g0 + 2 * m] * jnp.exp(b[g0 + m:g0 + 2 * m] - br))
        zs.append(jnp.concatenate(parts, axis=0))
        m *= 2
    return [z.astype(_BF16) for z in zs]


def _mixer_kernel(x_ref, nw_ref, w_in_ref, dwwb_ref, dwbb_ref, lnw_ref, lnb_ref, wco_ref, bco_ref,
                  lb_ref, hnw_ref, who_ref, wout_ref, lvl_ref, tri_ref, out_ref,
                  *scratch):
    ts = TOKEN_TILE
    ubuf, cbuf = scratch[0:CONV_SLABS], scratch[CONV_SLABS:2 * CONV_SLABS]
    state, obuf, bsc = scratch[2 * CONV_SLABS:]

    @pl.when(pl.program_id(1) == 0)
    def _():
        for c in range(CONV_SLABS):
            ubuf[c][0:CONV_HALO, :] = jnp.zeros((CONV_HALO, LANES), _F32)
        state[...] = jnp.zeros_like(state)

    x = x_ref[0]
    hb = _rms(x, nw_ref[...]).astype(_BF16)

    ag = _dot(hb, w_in_ref[:, 0:2 * CONV_WIDTH])
    a_half = ag[:, :CONV_WIDTH]
    u = a_half + a_half * jnp.tanh(ag[:, CONV_WIDTH:])
    never = lvl_ref[0:SUBLANES, :] == NEVER_LEVEL
    c = _causal_conv(u, dwwb_ref, dwbb_ref, ubuf, cbuf, never)
    hp = _dot(hb, w_in_ref[:, 2 * CONV_WIDTH:2 * CONV_WIDTH + 4 * HG_WIDTH])
    gates = _dot(hb, w_in_ref[:, 2 * CONV_WIDTH + 4 * HG_WIDTH:])
    mu = jnp.mean(c, axis=-1, keepdims=True)
    cc = c - mu
    var = jnp.mean(cc * cc, axis=-1, keepdims=True)
    cn_half = cc * lax.rsqrt(var + LN_EPS) * (0.5 * lnw_ref[...]) + 0.5 * lnb_ref[...]
    cn = cn_half + cn_half * jnp.tanh(cn_half)
    y_conv = _dot(cn.astype(_BF16), wco_ref[...]) + bco_ref[...]

    lbp = lb_ref[...]
    lmax = jnp.max(lbp, axis=0, keepdims=True)
    lexp = jnp.exp(lbp - lmax)
    lb = lexp[0:1, :] / jnp.sum(lexp, axis=0, keepdims=True)
    tri = tri_ref[...]
    lvl = lvl_ref[...]
    for r0 in range(0, ts, CHUNK):
        q = hp[r0:r0 + CHUNK, 0:HG_WIDTH]
        f_logit = hp[r0:r0 + CHUNK, HG_WIDTH:2 * HG_WIDTH]
        v = hp[r0:r0 + CHUNK, 2 * HG_WIDTH:3 * HG_WIDTH]
        g_out_half = hp[r0:r0 + CHUNK, 3 * HG_WIDTH:4 * HG_WIDTH]
        f = lb + (1.0 - lb) * jax.nn.sigmoid(f_logit)
        kk = 1.0 - f
        vb = v.astype(_BF16)

        g = jnp.log(f)
        g_hi = g.astype(_BF16)
        r1 = g - g_hi.astype(_F32)
        g_mid = r1.astype(_BF16)
        g_lo = (r1 - g_mid.astype(_F32)).astype(_BF16)
        b = _dot(tri, g_hi) + (_dot(tri, g_mid) + _dot(tri, g_lo))

        bsc[...] = b
        zs = _level_operands(q, kk, f, b, bsc)
        b_last = bsc[CHUNK - 1:CHUNK, :]
        q_in = (q * jnp.exp(b)).astype(_BF16)
        k_out = (kk * jnp.exp(b_last - b)).astype(_BF16)
        decay_all = jnp.exp(b_last)
        qk = q * kk
        n_diag_levels = len(zs) - 1
        for hh in range(HG_HEADS):
            cols = slice(hh * HG_HEAD_DIM, (hh + 1) * HG_HEAD_DIM)
            diag = []
            for rows in (slice(0, HALF), slice(HALF, CHUNK)):
                acc = jnp.zeros((HALF, HALF), _F32)
                for l in range(n_diag_levels):
                    z = zs[l][rows, cols]
                    acc = jnp.where(lvl == l, _dot_nt(z, z), acc)
                diag.append(acc.astype(_BF16))
            low = _dot_nt(zs[-1][HALF:CHUNK, cols], zs[-1][0:HALF, cols]).astype(_BF16)
            st = state[hh]
            o = _dot_nt(q_in[:, cols], st.astype(_BF16))
            o_a = _dot(diag[0], vb[0:HALF, cols])
            o_b = _dot(jnp.concatenate([low, diag[1]], axis=1), vb[:, cols])
            o = o + jnp.concatenate([o_a, o_b], axis=0)
            dg = jnp.sum(qk[:, cols], axis=-1, keepdims=True)
            o = o + dg * v[:, cols]
            vt = v[:, cols].T.astype(_BF16)
            state[hh] = st * decay_all[:, cols] + _dot(vt, k_out[:, cols])
            o = o * lax.rsqrt(jnp.mean(o * o, axis=-1, keepdims=True) + NORM_EPS)
            o = o * hnw_ref[:, cols]
            gh = g_out_half[:, cols]
            obuf[r0:r0 + CHUNK, cols] = (o * (gh + gh * jnp.tanh(gh))).astype(_BF16)
    y_rec = _dot(obuf[...], who_ref[...])

    y2 = (y_conv + y_rec) + (jnp.tanh(gates[:, :D_MODEL]) * y_conv + jnp.tanh(gates[:, D_MODEL:]) * y_rec)
    out_ref[0] = x + _dot(y2.astype(_BF16), wout_ref[...])


def _mlp_kernel(x_ref, nw_ref, wup_ref, wdn_ref, fw_ref, out_ref, acc):
    x = x_ref[...]
    hb = _rms(x, nw_ref[...]).astype(_BF16)
    for ci in range(D_FF // FF_CHUNK):
        lo = ci * FF_CHUNK
        up = jnp.maximum(_dot(hb, wup_ref[:, lo:lo + FF_CHUNK]), 0.0)
        part = _dot((up * up).astype(_BF16), wdn_ref[lo:lo + FF_CHUNK, :])
        if ci == 0:
            acc[...] = x + part
        else:
            acc[...] += part
    out_ref[...] = _rms(acc[...], fw_ref[...])


def _const_spec(shape):
    nd = len(shape)
    return pl.BlockSpec(shape, lambda *_: (0,) * nd, pipeline_mode=pl.Buffered(1))


def _slabs(a):
    return a.reshape(a.shape[0], CONV_SLABS, LANES).transpose(1, 0, 2)


@jax.jit
def kernel(x, norm_mix_w, w_in, dw_conv_w, dw_conv_b, conv_ln_w, conv_ln_b, w_conv_out, b_conv_out, hgrn_lb, hgrn_norm_w, w_hgrn_out, w_out, norm_mlp_w, w_mlp_up, w_mlp_down, norm_final_w):
    batch, seq, d = x.shape
    assert d == D_MODEL and seq % TOKEN_TILE == 0 and TOKEN_TILE % CHUNK == 0 and (batch * seq) % MLP_TILE == 0
    ts = TOKEN_TILE
    row2 = lambda a: a.reshape(1, -1).astype(_F32)

    dww = dw_conv_w[0].reshape(CONV_KERNEL, CONV_WIDTH).astype(_F32)
    dwwb = _slabs(jnp.repeat(dww, SUBLANES, axis=0))
    dwbb = _slabs(jnp.broadcast_to(dw_conv_b[0].astype(_F32), (SUBLANES, CONV_WIDTH)))

    col = np.arange(w_in.shape[-1])
    qf_cols = (col >= 2 * CONV_WIDTH) & (col < 2 * CONV_WIDTH + 2 * HG_WIDTH)
    v_cols = (col >= 2 * CONV_WIDTH + 2 * HG_WIDTH) & (col < 2 * CONV_WIDTH + 3 * HG_WIDTH)
    in_scale = jnp.asarray(np.where(qf_cols | v_cols, 1.0, 0.5), _F32)
    mixer_consts = [
        row2(norm_mix_w[0]),
        (w_in[0] * in_scale).astype(_BF16),
        dwwb, dwbb, row2(conv_ln_w[0]), row2(conv_ln_b[0]),
        w_conv_out[0].astype(_BF16), row2(b_conv_out[0]),
        hgrn_lb.astype(_F32), row2(hgrn_norm_w[0]),
        w_hgrn_out[0].astype(_BF16), (0.5 * w_out[0]).astype(_BF16),
        jnp.asarray(_pair_levels(HALF)),
        jnp.asarray(np.tril(np.ones((CHUNK, CHUNK), np.float32)), dtype=_BF16),
    ]
    x1 = pl.pallas_call(
        _mixer_kernel,
        grid=(batch, seq // ts),
        in_specs=[pl.BlockSpec((1, ts, d), lambda b, t: (b, t, 0))] + [_const_spec(a.shape) for a in mixer_consts],
        out_specs=pl.BlockSpec((1, ts, d), lambda b, t: (b, t, 0)),
        out_shape=jax.ShapeDtypeStruct(x.shape, _F32),
        scratch_shapes=[pltpu.VMEM((ts + CONV_HALO, LANES), _F32)] * CONV_SLABS
        + [pltpu.VMEM((ts, LANES), _F32)] * CONV_SLABS
        + [
            pltpu.VMEM((HG_HEADS, HG_HEAD_DIM, HG_HEAD_DIM), _F32),
            pltpu.VMEM((ts, HG_WIDTH), _BF16),
            pltpu.VMEM((CHUNK, HG_WIDTH), _F32),
        ],
        compiler_params=pltpu.CompilerParams(
            dimension_semantics=("arbitrary", "arbitrary"), vmem_limit_bytes=VMEM_LIMIT_BYTES),
        name="mixer",
    )(x, *mixer_consts)

    n_tok = batch * seq
    mlp_consts = [row2(norm_mlp_w[0]), w_mlp_up[0].astype(_BF16), w_mlp_down[0].astype(_BF16), row2(norm_final_w)]
    out = pl.pallas_call(
        _mlp_kernel,
        grid=(n_tok // MLP_TILE,),
        in_specs=[pl.BlockSpec((MLP_TILE, d), lambda i: (i, 0))] + [_const_spec(a.shape) for a in mlp_consts],
        out_specs=pl.BlockSpec((MLP_TILE, d), lambda i: (i, 0)),
        out_shape=jax.ShapeDtypeStruct((n_tok, d), _F32),
        scratch_shapes=[pltpu.VMEM((MLP_TILE, d), _F32)],
        compiler_params=pltpu.CompilerParams(
            dimension_semantics=("arbitrary",), vmem_limit_bytes=VMEM_LIMIT_BYTES),
        name="mlp",
    )(x1.reshape(n_tok, d), *mlp_consts)
    return out.reshape(batch, seq, d)
```

```python
import jax
import jax.numpy as jnp
import numpy as np
from jax import lax
from jax.experimental import pallas as pl
from jax.experimental.pallas import tpu as pltpu

D_MODEL = 1024
CONV_WIDTH = 512
CONV_KERNEL = 31
HG_WIDTH = 512
HG_HEAD_DIM = 128
HG_HEADS = HG_WIDTH // HG_HEAD_DIM
D_FF = 4096
NORM_EPS = 1e-6
LN_EPS = 1e-5

SUBLANES = 8
LANES = 128

TOKEN_TILE = 1024
CHUNK = 256
HALF = CHUNK // 2
CONV_HALO = 32
CONV_STRIDE = 4
CONV_ROW_BLOCK = SUBLANES * CONV_STRIDE
CONV_SLABS = CONV_WIDTH // LANES
CONV_CHAINS = 1
NEVER_LEVEL = 1 << 20
MLP_TILE = 1024
FF_CHUNK = 1024
VMEM_LIMIT_BYTES = 63 * 1024 * 1024

_BF16 = jnp.bfloat16
_F32 = jnp.float32


def _dot(a, b):
    return jnp.dot(a, b, preferred_element_type=_F32)


def _dot_nt(a, b):
    return lax.dot_general(a, b, (((1,), (1,)), ((), ())), preferred_element_type=_F32)


def _pair_levels(n):
    t = np.arange(n)[:, None]
    s = np.arange(n)[None, :]
    lvl = np.floor(np.log2(np.maximum(t ^ s, 1))).astype(np.int32)
    return np.where(t > s, lvl, -1).astype(np.int32)


def _rms(x, w):
    return x * lax.rsqrt(jnp.mean(x * x, axis=-1, keepdims=True) + NORM_EPS) * w


def _causal_conv(u, dwwb_ref, dwbb_ref, ubuf, cbuf, never):
    ts = TOKEN_TILE
    for c in range(CONV_SLABS):
        ubuf[c][CONV_HALO:CONV_HALO + ts, :] = u[:, c * LANES:(c + 1) * LANES]
    first_off = CONV_HALO - (CONV_KERNEL - 1)
    units = [(r0, c) for r0 in range(0, ts, CONV_ROW_BLOCK) for c in range(CONV_SLABS)]
    prev = [None] * CONV_CHAINS
    for n, (r0, c) in enumerate(units):
        chain = n % CONV_CHAINS
        bias = dwbb_ref[c]
        if prev[chain] is None:
            accs = [bias] * CONV_STRIDE
        else:
            accs = [jnp.where(never, p, bias) for p in prev[chain]]
        wins = {}
        for j in range(CONV_KERNEL):
            wj = dwwb_ref[c, j * SUBLANES:(j + 1) * SUBLANES, :]
            for t0 in range(CONV_STRIDE):
                start = r0 + t0 + first_off + j
                if start not in wins:
                    wins[start] = ubuf[c][pl.ds(start, SUBLANES, stride=CONV_STRIDE), :]
                accs[t0] = accs[t0] + wj * wins[start]
        for t0 in range(CONV_STRIDE):
            cbuf[c][pl.ds(r0 + t0, SUBLANES, stride=CONV_STRIDE), :] = accs[t0]
        prev[chain] = accs
    for c in range(CONV_SLABS):
        ubuf[c][0:CONV_HALO, :] = ubuf[c][ts:ts + CONV_HALO, :]
    return jnp.concatenate([cbuf[c][...] for c in range(CONV_SLABS)], axis=-1)


def _level_operands(q, kk, f, b, bsc):
    ts = q.shape[0]
    row = lax.broadcasted_iota(jnp.int32, (ts, HG_WIDTH), 0)
    qf = q * f
    zs = []
    zs.append(jnp.where((row & 1) != 0, qf, kk))
    p = row & 3
    f_next = pltpu.roll(f, ts - 1, 0)
    f_prev = pltpu.roll(f, 1, 0)
    zs.append(jnp.where(p < 2, kk * jnp.where(p == 0, f_next, 1.0), qf * jnp.where(p == 3, f_prev, 1.0)))
    sub = lax.broadcasted_iota(jnp.int32, (SUBLANES, HG_WIDTH), 0)
    parts = []
    for g0 in range(0, ts, SUBLANES):
        bg = b[g0:g0 + SUBLANES]
        d = bg - bsc[g0 + 3:g0 + 4, :]
        e = jnp.exp(jnp.where(sub < 4, -d, d))
        parts.append(jnp.where(sub < 4, kk[g0:g0 + SUBLANES], q[g0:g0 + SUBLANES]) * e)
    zs.append(jnp.concatenate(parts, axis=0))
    m = 8
    while m < ts:
        parts = []
        for g0 in range(0, ts, 2 * m):
            br = bsc[g0 + m - 1:g0 + m, :]
            parts.append(kk[g0:g0 + m] * jnp.exp(br - b[g0:g0 + m]))
            parts.append(q[g0 + m:g0 + 2 * m] * jnp.exp(b[g0 + m:g0 + 2 * m] - br))
        zs.append(jnp.concatenate(parts, axis=0))
        m *= 2
    return [z.astype(_BF16) for z in zs]


def _mixer_kernel(x_ref, nw_ref, w_in_ref, dwwb_ref, dwbb_ref, lnw_ref, lnb_ref, wco_ref, bco_ref,
                  lb_ref, hnw_ref, who_ref, wout_ref, lvl_ref, tri_ref, out_ref,
                  *scratch):
    ts = TOKEN_TILE
    ubuf, cbuf = scratch[0:CONV_SLABS], scratch[CONV_SLABS:2 * CONV_SLABS]
    state, obuf, bsc = scratch[2 * CONV_SLABS:]

    @pl.when(pl.program_id(1) == 0)
    def _():
        for c in range(CONV_SLABS):
            ubuf[c][0:CONV_HALO, :] = jnp.zeros((CONV_HALO, LANES), _F32)
        state[...] = jnp.zeros_like(state)

    hb = _rms(x_ref[0], nw_ref[...]).astype(_BF16)

    ag = _dot(hb, w_in_ref[:, 0:2 * CONV_WIDTH])
    a_half = ag[:, :CONV_WIDTH]
    u = a_half + a_half * jnp.tanh(ag[:, CONV_WIDTH:])
    never = lvl_ref[0:SUBLANES, :] == NEVER_LEVEL
    c = _causal_conv(u, dwwb_ref, dwbb_ref, ubuf, cbuf, never)
    hp = _dot(hb, w_in_ref[:, 2 * CONV_WIDTH:2 * CONV_WIDTH + 4 * HG_WIDTH])
    gates = _dot(hb, w_in_ref[:, 2 * CONV_WIDTH + 4 * HG_WIDTH:])
    mu = jnp.mean(c, axis=-1, keepdims=True)
    cc = c - mu
    var = jnp.mean(cc * cc, axis=-1, keepdims=True)
    cn_half = cc * lax.rsqrt(var + LN_EPS) * (0.5 * lnw_ref[...]) + 0.5 * lnb_ref[...]
    cn = cn_half + cn_half * jnp.tanh(cn_half)
    y_conv = _dot(cn.astype(_BF16), wco_ref[...]) + bco_ref[...]

    lbp = lb_ref[...]
    lmax = jnp.max(lbp, axis=0, keepdims=True)
    lexp = jnp.exp(lbp - lmax)
    lb = lexp[0:1, :] / jnp.sum(lexp, axis=0, keepdims=True)
    tri = tri_ref[...]
    lvl = lvl_ref[...]
    for r0 in range(0, ts, CHUNK):
        q = hp[r0:r0 + CHUNK, 0:HG_WIDTH]
        f_logit = hp[r0:r0 + CHUNK, HG_WIDTH:2 * HG_WIDTH]
        v = hp[r0:r0 + CHUNK, 2 * HG_WIDTH:3 * HG_WIDTH]
        g_out_half = hp[r0:r0 + CHUNK, 3 * HG_WIDTH:4 * HG_WIDTH]
        f = lb + (1.0 - lb) * jax.nn.sigmoid(f_logit)
        kk = 1.0 - f
        vb = v.astype(_BF16)

        g = jnp.log(f)
        g_hi = g.astype(_BF16)
        g_lo = (g - g_hi.astype(_F32)).astype(_BF16)
        b = _dot(tri, g_hi) + _dot(tri, g_lo)

        bsc[...] = b
        zs = _level_operands(q, kk, f, b, bsc)
        b_last = bsc[CHUNK - 1:CHUNK, :]
        q_in = (q * jnp.exp(b)).astype(_BF16)
        k_out = (kk * jnp.exp(b_last - b)).astype(_BF16)
        decay_all = jnp.exp(b_last)
        qk = q * kk
        n_diag_levels = len(zs) - 1
        for hh in range(HG_HEADS):
            cols = slice(hh * HG_HEAD_DIM, (hh + 1) * HG_HEAD_DIM)
            diag = []
            for rows in (slice(0, HALF), slice(HALF, CHUNK)):
                acc = jnp.zeros((HALF, HALF), _F32)
                for l in range(n_diag_levels):
                    z = zs[l][rows, cols]
                    acc = jnp.where(lvl == l, _dot_nt(z, z), acc)
                diag.append(acc.astype(_BF16))
            low = _dot_nt(zs[-1][HALF:CHUNK, cols], zs[-1][0:HALF, cols]).astype(_BF16)
            st = state[hh]
            o = _dot_nt(q_in[:, cols], st.astype(_BF16))
            o_a = _dot(diag[0], vb[0:HALF, cols])
            o_b = _dot(jnp.concatenate([low, diag[1]], axis=1), vb[:, cols])
            o = o + jnp.concatenate([o_a, o_b], axis=0)
            dg = jnp.sum(qk[:, cols], axis=-1, keepdims=True)
            o = o + dg * v[:, cols]
            vt = v[:, cols].T.astype(_BF16)
            state[hh] = st * decay_all[:, cols] + _dot(vt, k_out[:, cols])
            o = o * lax.rsqrt(jnp.mean(o * o, axis=-1, keepdims=True) + NORM_EPS)
            o = o * hnw_ref[:, cols]
            gh = g_out_half[:, cols]
            obuf[r0:r0 + CHUNK, cols] = (o * (gh + gh * jnp.tanh(gh))).astype(_BF16)
    y_rec = _dot(obuf[...], who_ref[...])

    y2 = (y_conv + y_rec) + (jnp.tanh(gates[:, :D_MODEL]) * y_conv + jnp.tanh(gates[:, D_MODEL:]) * y_rec)
    out_ref[0] = x_ref[0] + _dot(y2.astype(_BF16), wout_ref[...])


def _mlp_kernel(x_ref, nw_ref, wup_ref, wdn_ref, fw_ref, out_ref, acc):
    x = x_ref[...]
    hb = _rms(x, nw_ref[...]).astype(_BF16)
    for ci in range(D_FF // FF_CHUNK):
        lo = ci * FF_CHUNK
        up = jnp.maximum(_dot(hb, wup_ref[:, lo:lo + FF_CHUNK]), 0.0)
        part = _dot((up * up).astype(_BF16), wdn_ref[lo:lo + FF_CHUNK, :])
        if ci == 0:
            acc[...] = x + part
        else:
            acc[...] += part
    out_ref[...] = _rms(acc[...], fw_ref[...])


def _const_spec(shape):
    nd = len(shape)
    return pl.BlockSpec(shape, lambda *_: (0,) * nd, pipeline_mode=pl.Buffered(1))


def _slabs(a):
    return a.reshape(a.shape[0], CONV_SLABS, LANES).transpose(1, 0, 2)


@jax.jit
def kernel(x, norm_mix_w, w_in, dw_conv_w, dw_conv_b, conv_ln_w, conv_ln_b, w_conv_out, b_conv_out, hgrn_lb, hgrn_norm_w, w_hgrn_out, w_out, norm_mlp_w, w_mlp_up, w_mlp_down, norm_final_w):
    batch, seq, d = x.shape
    assert d == D_MODEL and seq % TOKEN_TILE == 0 and TOKEN_TILE % CHUNK == 0 and (batch * seq) % MLP_TILE == 0
    ts = TOKEN_TILE
    row2 = lambda a: a.reshape(1, -1).astype(_F32)

    dww = dw_conv_w[0].reshape(CONV_KERNEL, CONV_WIDTH).astype(_F32)
    dwwb = _slabs(jnp.repeat(dww, SUBLANES, axis=0))
    dwbb = _slabs(jnp.broadcast_to(dw_conv_b[0].astype(_F32), (SUBLANES, CONV_WIDTH)))

    col = np.arange(w_in.shape[-1])
    qf_cols = (col >= 2 * CONV_WIDTH) & (col < 2 * CONV_WIDTH + 2 * HG_WIDTH)
    v_cols = (col >= 2 * CONV_WIDTH + 2 * HG_WIDTH) & (col < 2 * CONV_WIDTH + 3 * HG_WIDTH)
    in_scale = jnp.asarray(np.where(qf_cols | v_cols, 1.0, 0.5), _F32)
    mixer_consts = [
        row2(norm_mix_w[0]),
        (w_in[0] * in_scale).astype(_BF16),
        dwwb, dwbb, row2(conv_ln_w[0]), row2(conv_ln_b[0]),
        w_conv_out[0].astype(_BF16), row2(b_conv_out[0]),
        hgrn_lb.astype(_F32), row2(hgrn_norm_w[0]),
        w_hgrn_out[0].astype(_BF16), (0.5 * w_out[0]).astype(_BF16),
        jnp.asarray(_pair_levels(HALF)),
        jnp.asarray(np.tril(np.ones((CHUNK, CHUNK), np.float32)), dtype=_BF16),
    ]
    x1 = pl.pallas_call(
        _mixer_kernel,
        grid=(batch, seq // ts),
        in_specs=[pl.BlockSpec((1, ts, d), lambda b, t: (b, t, 0))] + [_const_spec(a.shape) for a in mixer_consts],
        out_specs=pl.BlockSpec((1, ts, d), lambda b, t: (b, t, 0)),
        out_shape=jax.ShapeDtypeStruct(x.shape, _F32),
        scratch_shapes=[pltpu.VMEM((ts + CONV_HALO, LANES), _F32)] * CONV_SLABS
        + [pltpu.VMEM((ts, LANES), _F32)] * CONV_SLABS
        + [
            pltpu.VMEM((HG_HEADS, HG_HEAD_DIM, HG_HEAD_DIM), _F32),
            pltpu.VMEM((ts, HG_WIDTH), _BF16),
            pltpu.VMEM((CHUNK, HG_WIDTH), _F32),
        ],
        compiler_params=pltpu.CompilerParams(
            dimension_semantics=("arbitrary", "arbitrary"), vmem_limit_bytes=VMEM_LIMIT_BYTES),
        name="mixer",
    )(x, *mixer_consts)

    n_tok = batch * seq
    mlp_consts = [row2(norm_mlp_w[0]), w_mlp_up[0].astype(_BF16), w_mlp_down[0].astype(_BF16), row2(norm_final_w)]
    out = pl.pallas_call(
        _mlp_kernel,
        grid=(n_tok // MLP_TILE,),
        in_specs=[pl.BlockSpec((MLP_TILE, d), lambda i: (i, 0))] + [_const_spec(a.shape) for a in mlp_consts],
        out_specs=pl.BlockSpec((MLP_TILE, d), lambda i: (i, 0)),
        out_shape=jax.ShapeDtypeStruct((n_tok, d), _F32),
        scratch_shapes=[pltpu.VMEM((MLP_TILE, d), _F32)],
        compiler_params=pltpu.CompilerParams(
            dimension_semantics=("arbitrary",), vmem_limit_bytes=VMEM_LIMIT_BYTES),
        name="mlp",
    )(x1.reshape(n_tok, d), *mlp_consts)
    return out.reshape(batch, seq, d)
```

```python
import jax
import jax.numpy as jnp
import numpy as np
from jax import lax
from jax.experimental import pallas as pl
from jax.experimental.pallas import tpu as pltpu

D_MODEL = 1024
CONV_WIDTH = 512
CONV_KERNEL = 31
HG_WIDTH = 512
HG_HEAD_DIM = 128
HG_HEADS = HG_WIDTH // HG_HEAD_DIM
D_FF = 4096
NORM_EPS = 1e-6
LN_EPS = 1e-5

SUBLANES = 8
LANES = 128

TOKEN_TILE = 1024
CHUNK = 256
HALF = CHUNK // 2
CONV_HALO = 32
CONV_STRIDE = 4
CONV_ROW_BLOCK = SUBLANES * CONV_STRIDE
CONV_SLABS = CONV_WIDTH // LANES
CONV_CHAINS = 1
NEVER_LEVEL = 1 << 20
MLP_TILE = 1024
FF_CHUNK = 1024
VMEM_LIMIT_BYTES = 63 * 1024 * 1024

_BF16 = jnp.bfloat16
_F32 = jnp.float32


def _dot(a, b):
    return jnp.dot(a, b, preferred_element_type=_F32)


def _dot_nt(a, b):
    return lax.dot_general(a, b, (((1,), (1,)), ((), ())), preferred_element_type=_F32)


def _pair_levels(n):
    t = np.arange(n)[:, None]
    s = np.arange(n)[None, :]
    lvl = np.floor(np.log2(np.maximum(t ^ s, 1))).astype(np.int32)
    return np.where(t > s, lvl, -1).astype(np.int32)


def _rms(x, w):
    return x * lax.rsqrt(jnp.mean(x * x, axis=-1, keepdims=True) + NORM_EPS) * w


def _causal_conv(u, dwwb_ref, dwbb_ref, ubuf, cbuf, never):
    ts = TOKEN_TILE
    for c in range(CONV_SLABS):
        ubuf[c][CONV_HALO:CONV_HALO + ts, :] = u[:, c * LANES:(c + 1) * LANES]
    first_off = CONV_HALO - (CONV_KERNEL - 1)
    units = [(r0, c) for r0 in range(0, ts, CONV_ROW_BLOCK) for c in range(CONV_SLABS)]
    prev = [None] * CONV_CHAINS
    for n, (r0, c) in enumerate(units):
        chain = n % CONV_CHAINS
        bias = dwbb_ref[c]
        if prev[chain] is None:
            accs = [bias] * CONV_STRIDE
        else:
            accs = [jnp.where(never, p, bias) for p in prev[chain]]
        wins = {}
        for j in range(CONV_KERNEL):
            wj = dwwb_ref[c, j * SUBLANES:(j + 1) * SUBLANES, :]
            for t0 in range(CONV_STRIDE):
                start = r0 + t0 + first_off + j
                if start not in wins:
                    wins[start] = ubuf[c][pl.ds(start, SUBLANES, stride=CONV_STRIDE), :]
                accs[t0] = accs[t0] + wj * wins[start]
        for t0 in range(CONV_STRIDE):
            cbuf[c][pl.ds(r0 + t0, SUBLANES, stride=CONV_STRIDE), :] = accs[t0]
        prev[chain] = accs
    for c in range(CONV_SLABS):
        ubuf[c][0:CONV_HALO, :] = ubuf[c][ts:ts + CONV_HALO, :]
    return jnp.concatenate([cbuf[c][...] for c in range(CONV_SLABS)], axis=-1)


def _level_operands(q, kk, f, b, bsc):
    ts = q.shape[0]
    row = lax.broadcasted_iota(jnp.int32, (ts, HG_WIDTH), 0)
    qf = q * f
    zs = []
    zs.append(jnp.where((row & 1) != 0, qf, kk))
    p = row & 3
    f_next = pltpu.roll(f, ts - 1, 0)
    f_prev = pltpu.roll(f, 1, 0)
    zs.append(jnp.where(p < 2, kk * jnp.where(p == 0, f_next, 1.0), qf * jnp.where(p == 3, f_prev, 1.0)))
    sub = lax.broadcasted_iota(jnp.int32, (SUBLANES, HG_WIDTH), 0)
    parts = []
    for g0 in range(0, ts, SUBLANES):
        bg = b[g0:g0 + SUBLANES]
        d = bg - bsc[g0 + 3:g0 + 4, :]
        e = jnp.exp(jnp.where(sub < 4, -d, d))
        parts.append(jnp.where(sub < 4, kk[g0:g0 + SUBLANES], q[g0:g0 + SUBLANES]) * e)
    zs.append(jnp.concatenate(parts, axis=0))
    m = 8
    while m < ts:
        parts = []
        for g0 in range(0, ts, 2 * m):
            br = bsc[g0 + m - 1:g0 + m, :]
            parts.append(kk[g0:g0 + m] * jnp.exp(br - b[g0:g0 + m]))
            parts.append(q[g0 + m:g0 + 2 * m] * jnp.exp(b[g0 + m:g0 + 2 * m] - br))
        zs.append(jnp.concatenate(parts, axis=0))
        m *= 2
    return [z.astype(_BF16) for z in zs]


def _diag_scores(z_levels, lvl):
    n = lvl.shape[0]
    rt = 2 * SUBLANES
    tiles = [jnp.zeros((SUBLANES, n), _F32) for _ in range(n // SUBLANES)]
    for l, z in enumerate(z_levels):
        m = 1 << l
        groups = [g for g in range(n // rt) if m < rt or (g * rt // m) % 2 == 1]
        lhs = z if m < rt else jnp.concatenate([z[g * rt:(g + 1) * rt] for g in groups], axis=0)
        s = _dot_nt(lhs, z)
        for i, g in enumerate(groups):
            for h in range(rt // SUBLANES):
                j = g * (rt // SUBLANES) + h
                if m < SUBLANES or (j * SUBLANES // m) % 2 == 1:
                    r0 = i * rt + h * SUBLANES
                    rows = slice(j * SUBLANES, (j + 1) * SUBLANES)
                    tiles[j] = jnp.where(lvl[rows] == l, s[r0:r0 + SUBLANES], tiles[j])
    return jnp.concatenate(tiles, axis=0)


def _mixer_kernel(x_ref, nw_ref, w_in_ref, dwwb_ref, dwbb_ref, lnw_ref, lnb_ref, wco_ref, bco_ref,
                  lb_ref, hnw_ref, who_ref, wout_ref, lvl_ref, tri_ref, out_ref,
                  *scratch):
    ts = TOKEN_TILE
    ubuf, cbuf = scratch[0:CONV_SLABS], scratch[CONV_SLABS:2 * CONV_SLABS]
    state, obuf, bsc = scratch[2 * CONV_SLABS:]

    @pl.when(pl.program_id(1) == 0)
    def _():
        for c in range(CONV_SLABS):
            ubuf[c][0:CONV_HALO, :] = jnp.zeros((CONV_HALO, LANES), _F32)
        state[...] = jnp.zeros_like(state)

    hb = _rms(x_ref[0], nw_ref[...]).astype(_BF16)

    ag = _dot(hb, w_in_ref[:, 0:2 * CONV_WIDTH])
    a_half = ag[:, :CONV_WIDTH]
    u = a_half + a_half * jnp.tanh(ag[:, CONV_WIDTH:])
    never = lvl_ref[0:SUBLANES, :] == NEVER_LEVEL
    c = _causal_conv(u, dwwb_ref, dwbb_ref, ubuf, cbuf, never)
    hp = _dot(hb, w_in_ref[:, 2 * CONV_WIDTH:2 * CONV_WIDTH + 4 * HG_WIDTH])
    gates = _dot(hb, w_in_ref[:, 2 * CONV_WIDTH + 4 * HG_WIDTH:])
    mu = jnp.mean(c, axis=-1, keepdims=True)
    cc = c - mu
    var = jnp.mean(cc * cc, axis=-1, keepdims=True)
    cn_half = cc * lax.rsqrt(var + LN_EPS) * (0.5 * lnw_ref[...]) + 0.5 * lnb_ref[...]
    cn = cn_half + cn_half * jnp.tanh(cn_half)
    y_conv = _dot(cn.astype(_BF16), wco_ref[...]) + bco_ref[...]

    lbp = lb_ref[...]
    lmax = jnp.max(lbp, axis=0, keepdims=True)
    lexp = jnp.exp(lbp - lmax)
    lb = lexp[0:1, :] / jnp.sum(lexp, axis=0, keepdims=True)
    tri = tri_ref[...]
    lvl = lvl_ref[...]
    for r0 in range(0, ts, CHUNK):
        q = hp[r0:r0 + CHUNK, 0:HG_WIDTH]
        f_logit = hp[r0:r0 + CHUNK, HG_WIDTH:2 * HG_WIDTH]
        v = hp[r0:r0 + CHUNK, 2 * HG_WIDTH:3 * HG_WIDTH]
        g_out_half = hp[r0:r0 + CHUNK, 3 * HG_WIDTH:4 * HG_WIDTH]
        f = lb + (1.0 - lb) * jax.nn.sigmoid(f_logit)
        kk = 1.0 - f
        vb = v.astype(_BF16)

        g = jnp.log(f)
        g_hi = g.astype(_BF16)
        g_lo = (g - g_hi.astype(_F32)).astype(_BF16)
        b = _dot(tri, g_hi) + _dot(tri, g_lo)

        bsc[...] = b
        zs = _level_operands(q, kk, f, b, bsc)
        b_last = bsc[CHUNK - 1:CHUNK, :]
        q_in = (q * jnp.exp(b)).astype(_BF16)
        k_out = (kk * jnp.exp(b_last - b)).astype(_BF16)
        decay_all = jnp.exp(b_last)
        qk = q * kk
        n_diag_levels = len(zs) - 1
        for hh in range(HG_HEADS):
            cols = slice(hh * HG_HEAD_DIM, (hh + 1) * HG_HEAD_DIM)
            diag = []
            for rows in (slice(0, HALF), slice(HALF, CHUNK)):
                diag.append(_diag_scores([zs[l][rows, cols] for l in range(n_diag_levels)], lvl).astype(_BF16))
            low = _dot_nt(zs[-1][HALF:CHUNK, cols], zs[-1][0:HALF, cols]).astype(_BF16)
            st = state[hh]
            o = _dot_nt(q_in[:, cols], st.astype(_BF16))
            o_a = _dot(diag[0], vb[0:HALF, cols])
            o_b = _dot(jnp.concatenate([low, diag[1]], axis=1), vb[:, cols])
            o = o + jnp.concatenate([o_a, o_b], axis=0)
            dg = jnp.sum(qk[:, cols], axis=-1, keepdims=True)
            o = o + dg * v[:, cols]
            vt = v[:, cols].T.astype(_BF16)
            state[hh] = st * decay_all[:, cols] + _dot(vt, k_out[:, cols])
            o = o * lax.rsqrt(jnp.mean(o * o, axis=-1, keepdims=True) + NORM_EPS)
            o = o * hnw_ref[:, cols]
            gh = g_out_half[:, cols]
            obuf[r0:r0 + CHUNK, cols] = (o * (gh + gh * jnp.tanh(gh))).astype(_BF16)
    y_rec = _dot(obuf[...], who_ref[...])

    y2 = (y_conv + y_rec) + (jnp.tanh(gates[:, :D_MODEL]) * y_conv + jnp.tanh(gates[:, D_MODEL:]) * y_rec)
    out_ref[0] = x_ref[0] + _dot(y2.astype(_BF16), wout_ref[...])


def _mlp_kernel(x_ref, nw_ref, wup_ref, wdn_ref, fw_ref, out_ref, acc):
    x = x_ref[...]
    hb = _rms(x, nw_ref[...]).astype(_BF16)
    for ci in range(D_FF // FF_CHUNK):
        lo = ci * FF_CHUNK
        up = jnp.maximum(_dot(hb, wup_ref[:, lo:lo + FF_CHUNK]), 0.0)
        part = _dot((up * up).astype(_BF16), wdn_ref[lo:lo + FF_CHUNK, :])
        if ci == 0:
            acc[...] = x + part
        else:
            acc[...] += part
    out_ref[...] = _rms(acc[...], fw_ref[...])


def _const_spec(shape):
    nd = len(shape)
    return pl.BlockSpec(shape, lambda *_: (0,) * nd, pipeline_mode=pl.Buffered(1))


def _slabs(a):
    return a.reshape(a.shape[0], CONV_SLABS, LANES).transpose(1, 0, 2)


@jax.jit
def kernel(x, norm_mix_w, w_in, dw_conv_w, dw_conv_b, conv_ln_w, conv_ln_b, w_conv_out, b_conv_out, hgrn_lb, hgrn_norm_w, w_hgrn_out, w_out, norm_mlp_w, w_mlp_up, w_mlp_down, norm_final_w):
    batch, seq, d = x.shape
    assert d == D_MODEL and seq % TOKEN_TILE == 0 and TOKEN_TILE % CHUNK == 0 and (batch * seq) % MLP_TILE == 0
    ts = TOKEN_TILE
    row2 = lambda a: a.reshape(1, -1).astype(_F32)

    dww = dw_conv_w[0].reshape(CONV_KERNEL, CONV_WIDTH).astype(_F32)
    dwwb = _slabs(jnp.repeat(dww, SUBLANES, axis=0))
    dwbb = _slabs(jnp.broadcast_to(dw_conv_b[0].astype(_F32), (SUBLANES, CONV_WIDTH)))

    col = np.arange(w_in.shape[-1])
    qf_cols = (col >= 2 * CONV_WIDTH) & (col < 2 * CONV_WIDTH + 2 * HG_WIDTH)
    v_cols = (col >= 2 * CONV_WIDTH + 2 * HG_WIDTH) & (col < 2 * CONV_WIDTH + 3 * HG_WIDTH)
    in_scale = jnp.asarray(np.where(qf_cols | v_cols, 1.0, 0.5), _F32)
    mixer_consts = [
        row2(norm_mix_w[0]),
        (w_in[0] * in_scale).astype(_BF16),
        dwwb, dwbb, row2(conv_ln_w[0]), row2(conv_ln_b[0]),
        w_conv_out[0].astype(_BF16), row2(b_conv_out[0]),
        hgrn_lb.astype(_F32), row2(hgrn_norm_w[0]),
        w_hgrn_out[0].astype(_BF16), (0.5 * w_out[0]).astype(_BF16),
        jnp.asarray(_pair_levels(HALF)),
        jnp.asarray(np.tril(np.ones((CHUNK, CHUNK), np.float32)), dtype=_BF16),
    ]
    x1 = pl.pallas_call(
        _mixer_kernel,
        grid=(batch, seq // ts),
        in_specs=[pl.BlockSpec((1, ts, d), lambda b, t: (b, t, 0))] + [_const_spec(a.shape) for a in mixer_consts],
        out_specs=pl.BlockSpec((1, ts, d), lambda b, t: (b, t, 0)),
        out_shape=jax.ShapeDtypeStruct(x.shape, _F32),
        scratch_shapes=[pltpu.VMEM((ts + CONV_HALO, LANES), _F32)] * CONV_SLABS
        + [pltpu.VMEM((ts, LANES), _F32)] * CONV_SLABS
        + [
            pltpu.VMEM((HG_HEADS, HG_HEAD_DIM, HG_HEAD_DIM), _F32),
            pltpu.VMEM((ts, HG_WIDTH), _BF16),
            pltpu.VMEM((CHUNK, HG_WIDTH), _F32),
        ],
        compiler_params=pltpu.CompilerParams(
            dimension_semantics=("arbitrary", "arbitrary"), vmem_limit_bytes=VMEM_LIMIT_BYTES),
        name="mixer",
    )(x, *mixer_consts)

    n_tok = batch * seq
    mlp_consts = [row2(norm_mlp_w[0]), w_mlp_up[0].astype(_BF16), w_mlp_down[0].astype(_BF16), row2(norm_final_w)]
    out = pl.pallas_call(
        _mlp_kernel,
        grid=(n_tok // MLP_TILE,),
        in_specs=[pl.BlockSpec((MLP_TILE, d), lambda i: (i, 0))] + [_const_spec(a.shape) for a in mlp_consts],
        out_specs=pl.BlockSpec((MLP_TILE, d), lambda i: (i, 0)),
        out_shape=jax.ShapeDtypeStruct((n_tok, d), _F32),
        scratch_shapes=[pltpu.VMEM((MLP_TILE, d), _F32)],
        compiler_params=pltpu.CompilerParams(
            dimension_semantics=("arbitrary",), vmem_limit_bytes=VMEM_LIMIT_BYTES),
        name="mlp",
    )(x1.reshape(n_tok, d), *mlp_consts)
    return out.reshape(batch, seq, d)
```

```python
import jax
import jax.numpy as jnp
import numpy as np
from jax import lax
from jax.experimental import pallas as pl
from jax.experimental.pallas import tpu as pltpu

D_MODEL = 1024
CONV_WIDTH = 512
CONV_KERNEL = 31
HG_WIDTH = 512
HG_HEAD_DIM = 128
HG_HEADS = HG_WIDTH // HG_HEAD_DIM
D_FF = 4096
NORM_EPS = 1e-6
LN_EPS = 1e-5

SUBLANES = 8
LANES = 128

TOKEN_TILE = 1024
CHUNK = 256
HALF = CHUNK // 2
CONV_HALO = 32
CONV_STRIDE = 4
CONV_ROW_BLOCK = SUBLANES * CONV_STRIDE
CONV_SLABS = CONV_WIDTH // LANES
CONV_CHAINS = 1
NEVER_LEVEL = 1 << 20
MLP_TILE = 1024
FF_CHUNK = 1024
VMEM_LIMIT_BYTES = 63 * 1024 * 1024

_BF16 = jnp.bfloat16
_F32 = jnp.float32


def _dot(a, b):
    return jnp.dot(a, b, preferred_element_type=_F32)


def _dot_nt(a, b):
    return lax.dot_general(a, b, (((1,), (1,)), ((), ())), preferred_element_type=_F32)


def _pair_levels(n):
    t = np.arange(n)[:, None]
    s = np.arange(n)[None, :]
    lvl = np.floor(np.log2(np.maximum(t ^ s, 1))).astype(np.int32)
    return np.where(t > s, lvl, -1).astype(np.int32)


def _rms(x, w):
    return x * lax.rsqrt(jnp.mean(x * x, axis=-1, keepdims=True) + NORM_EPS) * w


def _causal_conv(u, dwwb_ref, dwbb_ref, ubuf, cbuf, never):
    ts = TOKEN_TILE
    for c in range(CONV_SLABS):
        ubuf[c][CONV_HALO:CONV_HALO + ts, :] = u[:, c * LANES:(c + 1) * LANES]
    first_off = CONV_HALO - (CONV_KERNEL - 1)
    units = [(r0, c) for r0 in range(0, ts, CONV_ROW_BLOCK) for c in range(CONV_SLABS)]
    prev = [None] * CONV_CHAINS
    for n, (r0, c) in enumerate(units):
        chain = n % CONV_CHAINS
        bias = dwbb_ref[c]
        if prev[chain] is None:
            accs = [bias] * CONV_STRIDE
        else:
            accs = [jnp.where(never, p, bias) for p in prev[chain]]
        wins = {}
        for j in range(CONV_KERNEL):
            wj = dwwb_ref[c, j * SUBLANES:(j + 1) * SUBLANES, :]
            for t0 in range(CONV_STRIDE):
                start = r0 + t0 + first_off + j
                if start not in wins:
                    wins[start] = ubuf[c][pl.ds(start, SUBLANES, stride=CONV_STRIDE), :]
                accs[t0] = accs[t0] + wj * wins[start]
        for t0 in range(CONV_STRIDE):
            cbuf[c][pl.ds(r0 + t0, SUBLANES, stride=CONV_STRIDE), :] = accs[t0]
        prev[chain] = accs
    for c in range(CONV_SLABS):
        ubuf[c][0:CONV_HALO, :] = ubuf[c][ts:ts + CONV_HALO, :]
    return jnp.concatenate([cbuf[c][...] for c in range(CONV_SLABS)], axis=-1)


def _level_operands(q, kk, f, b, bsc, cols):
    ts, width = q.shape
    sub = lax.broadcasted_iota(jnp.int32, (SUBLANES, width), 0)
    odd, p = (sub & 1) != 0, sub & 3
    p_key, p_first, p_last, low_half = p < 2, p == 0, p == 3, sub < 4
    parts = [[], [], []]
    for g0 in range(0, ts, SUBLANES):
        rows = slice(g0, g0 + SUBLANES)
        qg, kg, fg, bg = q[rows], kk[rows], f[rows], b[rows]
        qf = qg * fg
        parts[0].append(jnp.where(odd, qf, kg))
        f_next = pltpu.roll(fg, SUBLANES - 1, 0)
        f_prev = pltpu.roll(fg, 1, 0)
        parts[1].append(jnp.where(p_key, kg * jnp.where(p_first, f_next, 1.0), qf * jnp.where(p_last, f_prev, 1.0)))
        d = bg - bsc[g0 + 3:g0 + 4, cols]
        parts[2].append(jnp.where(low_half, kg, qg) * jnp.exp(jnp.where(low_half, -d, d)))
    zs = [jnp.concatenate(pp, axis=0) for pp in parts]
    m = 8
    while m < ts:
        parts = []
        for g0 in range(0, ts, 2 * m):
            br = bsc[g0 + m - 1:g0 + m, cols]
            parts.append(kk[g0:g0 + m] * jnp.exp(br - b[g0:g0 + m]))
            parts.append(q[g0 + m:g0 + 2 * m] * jnp.exp(b[g0 + m:g0 + 2 * m] - br))
        zs.append(jnp.concatenate(parts, axis=0))
        m *= 2
    return [z.astype(_BF16) for z in zs]


def _diag_scores(z_levels, lvl):
    n = lvl.shape[0]
    rt = 2 * SUBLANES
    tiles = [jnp.zeros((SUBLANES, n), _F32) for _ in range(n // SUBLANES)]
    for l, z in enumerate(z_levels):
        m = 1 << l
        groups = [g for g in range(n // rt) if m < rt or (g * rt // m) % 2 == 1]
        lhs = z if m < rt else jnp.concatenate([z[g * rt:(g + 1) * rt] for g in groups], axis=0)
        s = _dot_nt(lhs, z)
        for i, g in enumerate(groups):
            for h in range(rt // SUBLANES):
                j = g * (rt // SUBLANES) + h
                if m < SUBLANES or (j * SUBLANES // m) % 2 == 1:
                    r0 = i * rt + h * SUBLANES
                    rows = slice(j * SUBLANES, (j + 1) * SUBLANES)
                    tiles[j] = jnp.where(lvl[rows] == l, s[r0:r0 + SUBLANES], tiles[j])
    return jnp.concatenate(tiles, axis=0)


def _mixer_kernel(x_ref, nw_ref, w_in_ref, dwwb_ref, dwbb_ref, lnw_ref, lnb_ref, wco_ref, bco_ref,
                  lb_ref, hnw_ref, who_ref, wout_ref, lvl_ref, tri_ref, out_ref,
                  *scratch):
    ts = TOKEN_TILE
    ubuf, cbuf = scratch[0:CONV_SLABS], scratch[CONV_SLABS:2 * CONV_SLABS]
    state, obuf, bsc = scratch[2 * CONV_SLABS:]

    @pl.when(pl.program_id(1) == 0)
    def _():
        for c in range(CONV_SLABS):
            ubuf[c][0:CONV_HALO, :] = jnp.zeros((CONV_HALO, LANES), _F32)
        state[...] = jnp.zeros_like(state)

    hb = _rms(x_ref[0], nw_ref[...]).astype(_BF16)

    ag = _dot(hb, w_in_ref[:, 0:2 * CONV_WIDTH])
    a_half = ag[:, :CONV_WIDTH]
    u = a_half + a_half * jnp.tanh(ag[:, CONV_WIDTH:])
    never = lvl_ref[0:SUBLANES, :] == NEVER_LEVEL
    c = _causal_conv(u, dwwb_ref, dwbb_ref, ubuf, cbuf, never)
    hp = _dot(hb, w_in_ref[:, 2 * CONV_WIDTH:2 * CONV_WIDTH + 4 * HG_WIDTH])
    gates = _dot(hb, w_in_ref[:, 2 * CONV_WIDTH + 4 * HG_WIDTH:])
    mu = jnp.mean(c, axis=-1, keepdims=True)
    cc = c - mu
    var = jnp.mean(cc * cc, axis=-1, keepdims=True)
    cn_half = cc * lax.rsqrt(var + LN_EPS) * (0.5 * lnw_ref[...]) + 0.5 * lnb_ref[...]
    cn = cn_half + cn_half * jnp.tanh(cn_half)
    y_conv = _dot(cn.astype(_BF16), wco_ref[...]) + bco_ref[...]

    lbp = lb_ref[...]
    lmax = jnp.max(lbp, axis=0, keepdims=True)
    lexp = jnp.exp(lbp - lmax)
    lb = lexp[0:1, :] / jnp.sum(lexp, axis=0, keepdims=True)
    tri = tri_ref[...]
    lvl = lvl_ref[...]
    for r0 in range(0, ts, CHUNK):
        q = hp[r0:r0 + CHUNK, 0:HG_WIDTH]
        f_logit = hp[r0:r0 + CHUNK, HG_WIDTH:2 * HG_WIDTH]
        v = hp[r0:r0 + CHUNK, 2 * HG_WIDTH:3 * HG_WIDTH]
        g_out_half = hp[r0:r0 + CHUNK, 3 * HG_WIDTH:4 * HG_WIDTH]
        f = lb + (1.0 - lb) * jax.nn.sigmoid(f_logit)
        kk = 1.0 - f
        vb = v.astype(_BF16)

        g = jnp.log(f)
        g_hi = g.astype(_BF16)
        g_lo = (g - g_hi.astype(_F32)).astype(_BF16)
        b = _dot(tri, g_hi) + _dot(tri, g_lo)

        bsc[...] = b
        b_last = bsc[CHUNK - 1:CHUNK, :]
        decay_all = jnp.exp(b_last)
        for hh in range(HG_HEADS):
            cols = slice(hh * HG_HEAD_DIM, (hh + 1) * HG_HEAD_DIM)
            qh, kh, bh = q[:, cols], kk[:, cols], b[:, cols]
            zs = _level_operands(qh, kh, f[:, cols], bh, bsc, cols)
            q_in = (qh * jnp.exp(bh)).astype(_BF16)
            k_out = (kh * jnp.exp(b_last[:, cols] - bh)).astype(_BF16)
            diag = []
            for rows in (slice(0, HALF), slice(HALF, CHUNK)):
                diag.append(_diag_scores([z[rows] for z in zs[:-1]], lvl).astype(_BF16))
            low = _dot_nt(zs[-1][HALF:CHUNK], zs[-1][0:HALF]).astype(_BF16)
            st = state[hh]
            o = _dot_nt(q_in, st.astype(_BF16))
            o_a = _dot(diag[0], vb[0:HALF, cols])
            o_b = _dot(jnp.concatenate([low, diag[1]], axis=1), vb[:, cols])
            o = o + jnp.concatenate([o_a, o_b], axis=0)
            dg = jnp.sum(qh * kh, axis=-1, keepdims=True)
            o = o + dg * v[:, cols]
            vt = v[:, cols].T.astype(_BF16)
            state[hh] = st * decay_all[:, cols] + _dot(vt, k_out)
            o = o * lax.rsqrt(jnp.mean(o * o, axis=-1, keepdims=True) + NORM_EPS)
            o = o * hnw_ref[:, cols]
            gh = g_out_half[:, cols]
            obuf[r0:r0 + CHUNK, cols] = (o * (gh + gh * jnp.tanh(gh))).astype(_BF16)
    y_rec = _dot(obuf[...], who_ref[...])

    y2 = (y_conv + y_rec) + (jnp.tanh(gates[:, :D_MODEL]) * y_conv + jnp.tanh(gates[:, D_MODEL:]) * y_rec)
    out_ref[0] = x_ref[0] + _dot(y2.astype(_BF16), wout_ref[...])


def _mlp_kernel(x_ref, nw_ref, wup_ref, wdn_ref, fw_ref, out_ref, acc):
    x = x_ref[...]
    hb = _rms(x, nw_ref[...]).astype(_BF16)
    for ci in range(D_FF // FF_CHUNK):
        lo = ci * FF_CHUNK
        up = jnp.maximum(_dot(hb, wup_ref[:, lo:lo + FF_CHUNK]), 0.0)
        part = _dot((up * up).astype(_BF16), wdn_ref[lo:lo + FF_CHUNK, :])
        if ci == 0:
            acc[...] = x + part
        else:
            acc[...] += part
    out_ref[...] = _rms(acc[...], fw_ref[...])


def _const_spec(shape):
    nd = len(shape)
    return pl.BlockSpec(shape, lambda *_: (0,) * nd, pipeline_mode=pl.Buffered(1))


def _slabs(a):
    return a.reshape(a.shape[0], CONV_SLABS, LANES).transpose(1, 0, 2)


@jax.jit
def kernel(x, norm_mix_w, w_in, dw_conv_w, dw_conv_b, conv_ln_w, conv_ln_b, w_conv_out, b_conv_out, hgrn_lb, hgrn_norm_w, w_hgrn_out, w_out, norm_mlp_w, w_mlp_up, w_mlp_down, norm_final_w):
    batch, seq, d = x.shape
    assert d == D_MODEL and seq % TOKEN_TILE == 0 and TOKEN_TILE % CHUNK == 0 and (batch * seq) % MLP_TILE == 0
    ts = TOKEN_TILE
    row2 = lambda a: a.reshape(1, -1).astype(_F32)

    dww = dw_conv_w[0].reshape(CONV_KERNEL, CONV_WIDTH).astype(_F32)
    dwwb = _slabs(jnp.repeat(dww, SUBLANES, axis=0))
    dwbb = _slabs(jnp.broadcast_to(dw_conv_b[0].astype(_F32), (SUBLANES, CONV_WIDTH)))

    col = np.arange(w_in.shape[-1])
    qf_cols = (col >= 2 * CONV_WIDTH) & (col < 2 * CONV_WIDTH + 2 * HG_WIDTH)
    v_cols = (col >= 2 * CONV_WIDTH + 2 * HG_WIDTH) & (col < 2 * CONV_WIDTH + 3 * HG_WIDTH)
    in_scale = jnp.asarray(np.where(qf_cols | v_cols, 1.0, 0.5), _F32)
    mixer_consts = [
        row2(norm_mix_w[0]),
        (w_in[0] * in_scale).astype(_BF16),
        dwwb, dwbb, row2(conv_ln_w[0]), row2(conv_ln_b[0]),
        w_conv_out[0].astype(_BF16), row2(b_conv_out[0]),
        hgrn_lb.astype(_F32), row2(hgrn_norm_w[0]),
        w_hgrn_out[0].astype(_BF16), (0.5 * w_out[0]).astype(_BF16),
        jnp.asarray(_pair_levels(HALF)),
        jnp.asarray(np.tril(np.ones((CHUNK, CHUNK), np.float32)), dtype=_BF16),
    ]
    x1 = pl.pallas_call(
        _mixer_kernel,
        grid=(batch, seq // ts),
        in_specs=[pl.BlockSpec((1, ts, d), lambda b, t: (b, t, 0))] + [_const_spec(a.shape) for a in mixer_consts],
        out_specs=pl.BlockSpec((1, ts, d), lambda b, t: (b, t, 0)),
        out_shape=jax.ShapeDtypeStruct(x.shape, _F32),
        scratch_shapes=[pltpu.VMEM((ts + CONV_HALO, LANES), _F32)] * CONV_SLABS
        + [pltpu.VMEM((ts, LANES), _F32)] * CONV_SLABS
        + [
            pltpu.VMEM((HG_HEADS, HG_HEAD_DIM, HG_HEAD_DIM), _F32),
            pltpu.VMEM((ts, HG_WIDTH), _BF16),
            pltpu.VMEM((CHUNK, HG_WIDTH), _F32),
        ],
        compiler_params=pltpu.CompilerParams(
            dimension_semantics=("arbitrary", "arbitrary"), vmem_limit_bytes=VMEM_LIMIT_BYTES),
        name="mixer",
    )(x, *mixer_consts)

    n_tok = batch * seq
    mlp_consts = [row2(norm_mlp_w[0]), w_mlp_up[0].astype(_BF16), w_mlp_down[0].astype(_BF16), row2(norm_final_w)]
    out = pl.pallas_call(
        _mlp_kernel,
        grid=(n_tok // MLP_TILE,),
        in_specs=[pl.BlockSpec((MLP_TILE, d), lambda i: (i, 0))] + [_const_spec(a.shape) for a in mlp_consts],
        out_specs=pl.BlockSpec((MLP_TILE, d), lambda i: (i, 0)),
        out_shape=jax.ShapeDtypeStruct((n_tok, d), _F32),
        scratch_shapes=[pltpu.VMEM((MLP_TILE, d), _F32)],
        compiler_params=pltpu.CompilerParams(
            dimension_semantics=("arbitrary",), vmem_limit_bytes=VMEM_LIMIT_BYTES),
        name="mlp",
    )(x1.reshape(n_tok, d), *mlp_consts)
    return out.reshape(batch, seq, d)
```
